```python
import jax, jax.numpy as jnp
from jax import lax
import numpy as np

D_MODEL = 2048
BATCH = 16
SEQ = 2048
DEPTH = 2
DEC_BATCH = 16
DEC_SEQ = 16
PAST_LEN = 1024

CHUNK = 64
N_MIXERS = 2
N_RGLRU_LAYERS = (DEPTH + 1) // 2
N_CONV_LAYERS = DEPTH // 2
N_DENSE_LAYERS = (DEPTH + 1) // 2
N_MOE_LAYERS = DEPTH // 2
LRU_WIDTH = D_MODEL
LRU_HEADS = 8
LRU_BLOCK = LRU_WIDTH // LRU_HEADS
LRU_CONV_W = 4
LRU_C = 8.0
CM_KERNEL = 31
D_FF = 5632
N_EXPERTS = 8
TOP_K = 2
MOE_D_FF = 2816
PLE_DIM = 256
LN_EPS = 1e-5
DEEPNORM_ALPHA = (2.0 * DEPTH) ** 0.25
DEEPNORM_BETA = (8.0 * DEPTH) ** -0.25

kernel_name = 'hybrid_rglru_conformer_stream_step'


def layer_norm(x, g, b):
    xf = x.astype(jnp.float32)
    mu = jnp.mean(xf, axis=-1, keepdims=True)
    var = jnp.mean(jnp.square(xf - mu), axis=-1, keepdims=True)
    y = (xf - mu) * lax.rsqrt(var + LN_EPS)
    return (y * g.astype(jnp.float32) + b.astype(jnp.float32)).astype(x.dtype)


def causal_depthwise_conv(x, buf, w, b):
    k = w.shape[0]
    xp = jnp.concatenate([buf.astype(x.dtype), x], axis=1)
    y = lax.conv_general_dilated(xp, w[:, None, :].astype(x.dtype), window_strides=(1,),
                                 padding='VALID', dimension_numbers=('NWC', 'WIO', 'NWC'),
                                 feature_group_count=x.shape[-1])
    return y + b, xp[:, -(k - 1):]


def rglru_block(u, conv_buf, h0, w_in, conv_w, conv_b, w_a, b_a, w_x, b_x, lam, w_out):
    bsz, t, _ = u.shape
    gate_br, rec_br = jnp.split(u @ w_in, 2, axis=-1)
    xc, new_buf = causal_depthwise_conv(rec_br, conv_buf, conv_w, conv_b)
    xh = xc.reshape(bsz, t, LRU_HEADS, LRU_BLOCK)
    r = jax.nn.sigmoid(jnp.einsum('bthi,hij->bthj', xh, w_a).reshape(bsz, t, LRU_WIDTH) + b_a)
    ig = jax.nn.sigmoid(jnp.einsum('bthi,hij->bthj', xh, w_x).reshape(bsz, t, LRU_WIDTH) + b_x)
    log_a = (-LRU_C * r.astype(jnp.float32)) * jax.nn.softplus(-lam.astype(jnp.float32))
    a = jnp.exp(log_a)
    mult = jnp.sqrt(-jnp.expm1(2.0 * log_a))
    bterm = mult * (ig * xc).astype(jnp.float32)
    bterm = bterm.at[:, 0].add(a[:, 0] * h0.astype(jnp.float32))

    def combine(left, right):
        a1, b1 = left
        a2, b2 = right
        return a1 * a2, a2 * b1 + b2

    _, h = lax.associative_scan(combine, (a, bterm), axis=1)
    y = (h.astype(u.dtype) * jax.nn.gelu(gate_br)) @ w_out
    return y, new_buf, h[:, -1].astype(u.dtype)


def conv_module(u, buf, w_pw1, b_pw1, dw_w, dw_b, ln_g, ln_b, w_pw2, b_pw2):
    val, gate = jnp.split(u @ w_pw1 + b_pw1, 2, axis=-1)
    v = val * jax.nn.sigmoid(gate)
    y, new_buf = causal_depthwise_conv(v, buf, dw_w, dw_b)
    y = jax.nn.silu(layer_norm(y, ln_g, ln_b))
    return y @ w_pw2 + b_pw2, new_buf


def swiglu(u, w_in, w_out):
    g, up = jnp.split(u @ w_in, 2, axis=-1)
    return (jax.nn.silu(g) * up) @ w_out


def moe_swiglu(u, w_router, w_in, w_out):
    probs = jax.nn.softmax((u @ w_router).astype(jnp.float32), axis=-1)
    top_p, top_i = lax.top_k(probs, TOP_K)
    top_p = top_p / jnp.sum(top_p, axis=-1, keepdims=True)
    gates = jnp.sum(jax.nn.one_hot(top_i, N_EXPERTS, dtype=jnp.float32) * top_p[..., None], axis=-2)
    out = jnp.zeros_like(u)
    for e in range(N_EXPERTS):
        out = out + gates[..., e:e + 1].astype(u.dtype) * swiglu(u, w_in[e], w_out[e])
    return out


def trunk(x, p, lru_conv, lru_h, cm_conv,
          rglru_w_in, rglru_conv_w, rglru_conv_b, rglru_w_a, rglru_b_a, rglru_w_x, rglru_b_x,
          rglru_lambda, rglru_w_out,
          cm_w_pw1, cm_b_pw1, cm_dw_w, cm_dw_b, cm_ln_g, cm_ln_b, cm_w_pw2, cm_b_pw2,
          ffn_w_in, ffn_w_out, moe_w_router, moe_w_in, moe_w_out,
          ln_mix_g, ln_mix_b, ln_ffn_g, ln_ffn_b, ple_w_proj, ple_w_gate, ple_b_gate):
    new_lru_conv, new_lru_h, new_cm_conv = [], [], []
    for i in range(DEPTH):
        j = i // N_MIXERS
        if i % N_MIXERS == 0:
            mix, buf, h = rglru_block(x, lru_conv[j], lru_h[j], rglru_w_in[j], rglru_conv_w[j],
                                      rglru_conv_b[j], rglru_w_a[j], rglru_b_a[j], rglru_w_x[j],
                                      rglru_b_x[j], rglru_lambda[j], rglru_w_out[j])
            new_lru_conv.append(buf)
            new_lru_h.append(h)
        else:
            mix, buf = conv_module(x, cm_conv[j], cm_w_pw1[j], cm_b_pw1[j], cm_dw_w[j], cm_dw_b[j],
                                   cm_ln_g[j], cm_ln_b[j], cm_w_pw2[j], cm_b_pw2[j])
            new_cm_conv.append(buf)
        x = layer_norm(DEEPNORM_ALPHA * x + mix, ln_mix_g[i], ln_mix_b[i])
        k = i // 2
        if i % 2 == 0:
            ff = swiglu(x, ffn_w_in[k], ffn_w_out[k])
        else:
            ff = moe_swiglu(x, moe_w_router[k], moe_w_in[k], moe_w_out[k])
        x = layer_norm(DEEPNORM_ALPHA * x + ff, ln_ffn_g[i], ln_ffn_b[i])
        x = x + jax.nn.sigmoid(x @ ple_w_gate[i] + ple_b_gate[i]) * (p[i] @ ple_w_proj[i])
    return x, jnp.stack(new_lru_conv), jnp.stack(new_lru_h), jnp.stack(new_cm_conv)


def setup_inputs(seed: int = 0) -> dict:
    key = jax.random.key(seed)
    keys = jax.random.split(key, 48)
    cnt = [0]

    def nxt():
        k = keys[cnt[0]]
        cnt[0] += 1
        return k

    def nrm(shape, scale):
        return scale * jax.random.normal(nxt(), shape, jnp.float32)

    NA, NB, ND, NM = N_RGLRU_LAYERS, N_CONV_LAYERS, N_DENSE_LAYERS, N_MOE_LAYERS
    D, LW = D_MODEL, LRU_WIDTH
    a0 = jax.random.uniform(nxt(), (NA, LW), jnp.float32, 0.9, 0.999)
    s = a0 ** (1.0 / LRU_C)
    lam = jnp.log(s) - jnp.log1p(-s)
    return {
        'x_prompt': nrm((BATCH, SEQ, D), 1.0),
        'x_sample': nrm((DEC_BATCH, DEC_SEQ, D), 1.0),
        'p_prompt': nrm((DEPTH, BATCH, SEQ, PLE_DIM), 1.0),
        'p_sample': nrm((DEPTH, DEC_BATCH, DEC_SEQ, PLE_DIM), 1.0),
        'state_rglru_conv': nrm((NA, DEC_BATCH, LRU_CONV_W - 1, LW), 1.0),
        'state_rglru_h': nrm((NA, DEC_BATCH, LW), 0.5),
        'state_conv_module': nrm((NB, DEC_BATCH, CM_KERNEL - 1, D), 0.5),
        'rglru_w_in': nrm((NA, D, 2 * LW), D ** -0.5),
        'rglru_conv_w': nrm((NA, LRU_CONV_W, LW), LRU_CONV_W ** -0.5),
        'rglru_conv_b': nrm((NA, LW), 0.01),
        'rglru_w_a': nrm((NA, LRU_HEADS, LRU_BLOCK, LRU_BLOCK), LRU_BLOCK ** -0.5),
        'rglru_b_a': nrm((NA, LW), 0.01),
        'rglru_w_x': nrm((NA, LRU_HEADS, LRU_BLOCK, LRU_BLOCK), LRU_BLOCK ** -0.5),
        'rglru_b_x': nrm((NA, LW), 0.01),
        'rglru_lambda': lam,
        'rglru_w_out': nrm((NA, LW, D), DEEPNORM_BETA * LW ** -0.5),
        'cm_w_pw1': nrm((NB, D, 2 * D), D ** -0.5),
        'cm_b_pw1': nrm((NB, 2 * D), 0.01),
        'cm_dw_w': nrm((NB, CM_KERNEL, D), CM_KERNEL ** -0.5),
        'cm_dw_b': nrm((NB, D), 0.01),
        'cm_ln_g': 1.0 + nrm((NB, D), 0.01),
        'cm_ln_b': nrm((NB, D), 0.01),
        'cm_w_pw2': nrm((NB, D, D), DEEPNORM_BETA * D ** -0.5),
        'cm_b_pw2': nrm((NB, D), 0.01),
        'ffn_w_in': nrm((ND, D, 2 * D_FF), D ** -0.5),
        'ffn_w_out': nrm((ND, D_FF, D), DEEPNORM_BETA * D_FF ** -0.5),
        'moe_w_router': nrm((NM, D, N_EXPERTS), D ** -0.5),
        'moe_w_in': nrm((NM, N_EXPERTS, D, 2 * MOE_D_FF), D ** -0.5),
        'moe_w_out': nrm((NM, N_EXPERTS, MOE_D_FF, D), DEEPNORM_BETA * MOE_D_FF ** -0.5),
        'ln_mix_g': 1.0 + nrm((DEPTH, D), 0.01),
        'ln_mix_b': nrm((DEPTH, D), 0.01),
        'ln_ffn_g': 1.0 + nrm((DEPTH, D), 0.01),
        'ln_ffn_b': nrm((DEPTH, D), 0.01),
        'ple_w_proj': nrm((DEPTH, PLE_DIM, D), 0.5 * PLE_DIM ** -0.5),
        'ple_w_gate': nrm((DEPTH, D, D), D ** -0.5),
        'ple_b_gate': nrm((DEPTH, D), 0.01),
    }


def reference(x_prompt, x_sample, p_prompt, p_sample, state_rglru_conv, state_rglru_h, state_conv_module,
              rglru_w_in, rglru_conv_w, rglru_conv_b, rglru_w_a, rglru_b_a, rglru_w_x, rglru_b_x,
              rglru_lambda, rglru_w_out,
              cm_w_pw1, cm_b_pw1, cm_dw_w, cm_dw_b, cm_ln_g, cm_ln_b, cm_w_pw2, cm_b_pw2,
              ffn_w_in, ffn_w_out, moe_w_router, moe_w_in, moe_w_out,
              ln_mix_g, ln_mix_b, ln_ffn_g, ln_ffn_b, ple_w_proj, ple_w_gate, ple_b_gate):
    weights = (rglru_w_in, rglru_conv_w, rglru_conv_b, rglru_w_a, rglru_b_a, rglru_w_x, rglru_b_x,
               rglru_lambda, rglru_w_out,
               cm_w_pw1, cm_b_pw1, cm_dw_w, cm_dw_b, cm_ln_g, cm_ln_b, cm_w_pw2, cm_b_pw2,
               ffn_w_in, ffn_w_out, moe_w_router, moe_w_in, moe_w_out,
               ln_mix_g, ln_mix_b, ln_ffn_g, ln_ffn_b, ple_w_proj, ple_w_gate, ple_b_gate)
    bp = x_prompt.shape[0]
    dt = x_prompt.dtype
    zero_conv = jnp.zeros((N_RGLRU_LAYERS, bp, LRU_CONV_W - 1, LRU_WIDTH), dt)
    zero_h = jnp.zeros((N_RGLRU_LAYERS, bp, LRU_WIDTH), dt)
    zero_cm = jnp.zeros((N_CONV_LAYERS, bp, CM_KERNEL - 1, D_MODEL), dt)
    y_prompt, conv_p, h_p, cm_p = trunk(x_prompt, p_prompt, zero_conv, zero_h, zero_cm, *weights)
    y_sample, conv_s, h_s, cm_s = trunk(x_sample, p_sample, state_rglru_conv, state_rglru_h,
                                        state_conv_module, *weights)
    return (y_prompt, y_sample, conv_p, h_p, cm_p, conv_s, h_s, cm_s)
```

```python
import functools

import jax
import jax.numpy as jnp
from jax import lax
from jax.experimental import pallas as pl
from jax.experimental.pallas import tpu as pltpu

_BF = jnp.bfloat16
_F32 = jnp.float32

LRU_C = 8.0
LN_EPS = 1e-5
TOP_K = 2

_V7X_VMEM_BYTES = 64 * 1024 * 1024
_VMEM_LIMIT = _V7X_VMEM_BYTES - 8 * 1024 * 1024
_SUBLANES = 8
_LANES = 128


def _cparams(*sem):
    return pltpu.CompilerParams(dimension_semantics=sem, vmem_limit_bytes=_VMEM_LIMIT)


def _pick(n, pref, mult=_SUBLANES):
    best = None
    for c in range(mult, min(n, pref) + 1, mult):
        if n % c == 0:
            best = c
    assert best is not None, (n, pref, mult)
    return best


def _resident(shape):
    nd = len(shape)
    return pl.BlockSpec(shape, lambda *_: (0,) * nd, pipeline_mode=pl.Buffered(1))


def _layer_norm(x, g, b):
    mu = jnp.mean(x, axis=-1, keepdims=True)
    xc = x - mu
    var = jnp.mean(xc * xc, axis=-1, keepdims=True)
    return xc * lax.rsqrt(var + LN_EPS) * g + b


def _sigmoid(x):
    return 1.0 / (1.0 + jnp.exp(-x))


def _gelu_tanh(x):
    c = 0.7978845608028654
    return 0.5 * x * (1.0 + jnp.tanh(c * (x + 0.044715 * (x * x * x))))


def _softplus(x):
    return jnp.maximum(x, 0.0) + jnp.log1p(jnp.exp(-jnp.abs(x)))


def _in_proj_kernel(x_ref, wg_ref, wr_ref, gate_ref, rec_ref, xb_scr):
    @pl.when(pl.program_id(1) == 0)
    def _():
        xb_scr[...] = x_ref[...].astype(_BF)

    xb = xb_scr[...]
    gate_ref[...] = _gelu_tanh(jnp.dot(xb, wg_ref[...], preferred_element_type=_F32))
    rec_ref[...] = jnp.dot(xb, wr_ref[...], preferred_element_type=_F32)


def _in_proj(x, w):
    n, d = x.shape
    lw = w.shape[1] // 2
    tm = _pick(n, 1024)
    tn = _pick(lw, 512, _LANES)
    nj = lw // tn
    return pl.pallas_call(
        _in_proj_kernel,
        grid=(n // tm, nj),
        in_specs=[
            pl.BlockSpec((tm, d), lambda i, j: (i, 0)),
            pl.BlockSpec((d, tn), lambda i, j: (0, j)),
            pl.BlockSpec((d, tn), lambda i, j: (0, j + nj)),
        ],
        out_specs=[
            pl.BlockSpec((tm, tn), lambda i, j: (i, j)),
            pl.BlockSpec((tm, tn), lambda i, j: (i, j)),
        ],
        out_shape=[jax.ShapeDtypeStruct((n, lw), _F32)] * 2,
        scratch_shapes=[pltpu.VMEM((tm, d), _BF)],
        compiler_params=_cparams("parallel", "arbitrary"),
        name="in_proj",
    )(x, w, w)


def _rglru_kernel(rec_ref, gate_ref, buf0_ref, h0_ref, cw_ref, cb_ref, wa_ref, ba_ref, wx_ref,
                  bx_ref, lam_ref, y_ref, nbuf_ref, hlast_ref, xp_scr, a_scr, b_scr, carry_scr,
                  *, tc, heads, kw):
    t = pl.program_id(1)
    lw = rec_ref.shape[-1]
    blk = lw // heads
    hist = _SUBLANES
    base = hist - (kw - 1)

    @pl.when(t == 0)
    def _():
        xp_scr[0:hist, :] = buf0_ref[0]
        carry_scr[...] = jnp.broadcast_to(h0_ref[0], (_SUBLANES, lw))

    xp_scr[hist:hist + tc, :] = rec_ref[0]
    decay = -LRU_C * _softplus(-lam_ref[...])
    for h in range(heads):
        sl = slice(h * blk, (h + 1) * blk)
        xc = cb_ref[:, sl] + cw_ref[0:1, sl] * xp_scr[base:base + tc, sl]
        for k in range(1, kw):
            xc = xc + cw_ref[k:k + 1, sl] * xp_scr[base + k:base + k + tc, sl]
        xcb = xc.astype(_BF)
        r = _sigmoid(jnp.dot(xcb, wa_ref[h], preferred_element_type=_F32) + ba_ref[:, sl])
        ig = _sigmoid(jnp.dot(xcb, wx_ref[h], preferred_element_type=_F32) + bx_ref[:, sl])
        log_a = r * decay[:, sl]
        a = jnp.exp(log_a)
        a_scr[:, sl] = a
        b_scr[:, sl] = jnp.sqrt(-jnp.tanh(log_a) * (a * a + 1.0)) * (ig * xc)

    row = lax.broadcasted_iota(jnp.int32, (_SUBLANES, lw), 0)

    def body(g, carry):
        r0 = pl.multiple_of(g * _SUBLANES, _SUBLANES)
        a = a_scr[pl.ds(r0, _SUBLANES), :]
        b = b_scr[pl.ds(r0, _SUBLANES), :]
        for s in (1, 2, 4):
            m = row >= s
            a_sh = jnp.where(m, pltpu.roll(a, s, 0), 1.0)
            b_sh = jnp.where(m, pltpu.roll(b, s, 0), 0.0)
            b = a * b_sh + b
            a = a * a_sh
        hrows = a * carry + b
        b_scr[pl.ds(r0, _SUBLANES), :] = hrows
        return jnp.broadcast_to(hrows[_SUBLANES - 1:_SUBLANES, :], (_SUBLANES, lw))

    carry = lax.fori_loop(0, tc // _SUBLANES, body, carry_scr[...])
    carry_scr[...] = carry
    y_ref[0] = (b_scr[...] * gate_ref[0]).astype(y_ref.dtype)
    hlast_ref[0] = carry
    tail = xp_scr[tc:tc + hist, :]
    nbuf_ref[0] = tail
    xp_scr[0:hist, :] = tail


def _rglru(rec, gate, buf0, h0, conv_w, conv_b, w_a, b_a, w_x, b_x, lam):
    bsz, t, lw = rec.shape
    heads, blk, _ = w_a.shape
    kw = conv_w.shape[0]
    assert kw - 1 <= _SUBLANES <= t and t % _SUBLANES == 0
    tc = _pick(t, 256)
    buf_pad = jnp.pad(buf0, ((0, 0), (_SUBLANES - (kw - 1), 0), (0, 0)))
    row = lambda v: v.reshape(1, lw)
    seq = pl.BlockSpec((1, tc, lw), lambda b, i: (b, i, 0))
    per_b = pl.BlockSpec((1, _SUBLANES, lw), lambda b, i: (b, 0, 0))
    y, nbuf, hlast = pl.pallas_call(
        functools.partial(_rglru_kernel, tc=tc, heads=heads, kw=kw),
        grid=(bsz, t // tc),
        in_specs=[
            seq, seq, per_b,
            pl.BlockSpec((1, 1, lw), lambda b, i: (b, 0, 0)),
            _resident((kw, lw)), _resident((1, lw)),
            _resident((heads, blk, blk)), _resident((1, lw)),
            _resident((heads, blk, blk)), _resident((1, lw)),
            _resident((1, lw)),
        ],
        out_specs=[seq, per_b, per_b],
        out_shape=[
            jax.ShapeDtypeStruct((bsz, t, lw), _BF),
            jax.ShapeDtypeStruct((bsz, _SUBLANES, lw), _F32),
            jax.ShapeDtypeStruct((bsz, _SUBLANES, lw), _F32),
        ],
        scratch_shapes=[
            pltpu.VMEM((_SUBLANES + tc, lw), _F32),
            pltpu.VMEM((tc, lw), _F32),
            pltpu.VMEM((tc, lw), _F32),
            pltpu.VMEM((_SUBLANES, lw), _F32),
        ],
        compiler_params=_cparams("parallel", "arbitrary"),
        name="rglru",
    )(rec, gate, buf_pad, h0.reshape(bsz, 1, lw), conv_w, row(conv_b), w_a, row(b_a), w_x,
      row(b_x), row(lam))
    return y, nbuf[:, _SUBLANES - (kw - 1):], hlast[:, 0]


def _proj_ln_kernel(a_ref, w_ref, bias_ref, x_ref, g_ref, b_ref, out_ref, outb_ref, *, alpha):
    mix = jnp.dot(a_ref[...], w_ref[...], preferred_element_type=_F32) + bias_ref[...]
    y = _layer_norm(alpha * x_ref[...] + mix, g_ref[...], b_ref[...])
    out_ref[...] = y
    outb_ref[...] = y.astype(_BF)


def _proj_ln(a, w, bias, x, g, b, alpha):
    n, k = a.shape
    d = w.shape[1]
    tm = _pick(n, 512)
    tile = lambda width: pl.BlockSpec((tm, width), lambda i: (i, 0))
    return pl.pallas_call(
        functools.partial(_proj_ln_kernel, alpha=alpha),
        grid=(n // tm,),
        in_specs=[tile(k), _resident((k, d)), _resident((1, d)), tile(d), _resident((1, d)),
                  _resident((1, d))],
        out_specs=[tile(d), tile(d)],
        out_shape=[jax.ShapeDtypeStruct((n, d), _F32), jax.ShapeDtypeStruct((n, d), _BF)],
        compiler_params=_cparams("parallel"),
        name="proj_ln",
    )(a, w, bias.reshape(1, d), x, g.reshape(1, d), b.reshape(1, d))


def _ffn_kernel(x_ref, wg_ref, wu_ref, wo_ref, *rest, gated):
    if gated:
        gates_ref, out_ref = rest
    else:
        (out_ref,) = rest
    e = pl.program_id(1)
    f = pl.program_id(2)

    @pl.when((e == 0) & (f == 0))
    def _():
        out_ref[...] = jnp.zeros_like(out_ref)

    xb = x_ref[...]
    g = jnp.dot(xb, wg_ref[0], preferred_element_type=_F32)
    u = jnp.dot(xb, wu_ref[0], preferred_element_type=_F32)
    h = (g * _sigmoid(g)) * u
    if gated:
        lane = lax.broadcasted_iota(jnp.int32, gates_ref.shape, 1)
        h = h * jnp.sum(jnp.where(lane == e, gates_ref[...], 0.0), axis=1, keepdims=True)
    out_ref[...] += jnp.dot(h.astype(_BF), wo_ref[0], preferred_element_type=_F32)


def _ffn(xb, w_in, w_out, gates=None, tf_pref=512):
    n, d = xb.shape
    ne, f, _ = w_out.shape
    tm = _pick(n, 512)
    tf = _pick(f, tf_pref, _LANES)
    nf = f // tf
    in_specs = [
        pl.BlockSpec((tm, d), lambda i, e, j: (i, 0)),
        pl.BlockSpec((1, d, tf), lambda i, e, j: (e, 0, j)),
        pl.BlockSpec((1, d, tf), lambda i, e, j: (e, 0, j + nf)),
        pl.BlockSpec((1, tf, d), lambda i, e, j: (e, j, 0)),
    ]
    args = [xb, w_in, w_in, w_out]
    if gates is not None:
        in_specs.append(pl.BlockSpec((tm, _LANES), lambda i, e, j: (i, 0)))
        args.append(gates)
    return pl.pallas_call(
        functools.partial(_ffn_kernel, gated=gates is not None),
        grid=(n // tm, ne, nf),
        in_specs=in_specs,
        out_specs=pl.BlockSpec((tm, d), lambda i, e, j: (i, 0)),
        out_shape=jax.ShapeDtypeStruct((n, d), _F32),
        compiler_params=_cparams("parallel", "arbitrary", "arbitrary"),
        name="ffn",
    )(*args)


def _post_kernel(x_ref, ff_ref, g_ref, b_ref, p_ref, wg_ref, bg_ref, wp_ref, out_ref, outb_ref,
                 *, alpha):
    xn = _layer_norm(alpha * x_ref[...] + ff_ref[...], g_ref[...], b_ref[...])
    gate = _sigmoid(jnp.dot(xn.astype(_BF), wg_ref[...], preferred_element_type=_F32)
                    + bg_ref[...])
    proj = jnp.dot(p_ref[...].astype(_BF), wp_ref[...], preferred_element_type=_F32)
    y = xn + gate * proj
    out_ref[...] = y
    outb_ref[...] = y.astype(_BF)


def _post(x, ff, g, b, p, w_gate, b_gate, w_proj, alpha):
    n, d = x.shape
    pd = p.shape[1]
    tm = _pick(n, 512)
    tile = lambda width: pl.BlockSpec((tm, width), lambda i: (i, 0))
    return pl.pallas_call(
        functools.partial(_post_kernel, alpha=alpha),
        grid=(n // tm,),
        in_specs=[tile(d), tile(d), _resident((1, d)), _resident((1, d)), tile(pd),
                  _resident((d, d)), _resident((1, d)), _resident((pd, d))],
        out_specs=[tile(d), tile(d)],
        out_shape=[jax.ShapeDtypeStruct((n, d), _F32), jax.ShapeDtypeStruct((n, d), _BF)],
        compiler_params=_cparams("parallel"),
        name="post",
    )(x, ff, g.reshape(1, d), b.reshape(1, d), p, w_gate, b_gate.reshape(1, d), w_proj)


def _pw1_glu_kernel(x_ref, wv_ref, wg_ref, bv_ref, bg_ref, v_ref):
    xb = x_ref[...]
    val = jnp.dot(xb, wv_ref[...], preferred_element_type=_F32) + bv_ref[...]
    gate = jnp.dot(xb, wg_ref[...], preferred_element_type=_F32) + bg_ref[...]
    v_ref[...] = val * _sigmoid(gate)


def _pw1_glu(xb, w, bias):
    n, d = xb.shape
    dv = w.shape[1] // 2
    tm = _pick(n, 1024)
    tn = _pick(dv, 512, _LANES)
    nj = dv // tn
    bias = bias.reshape(1, 2 * dv)
    return pl.pallas_call(
        _pw1_glu_kernel,
        grid=(n // tm, nj),
        in_specs=[
            pl.BlockSpec((tm, d), lambda i, j: (i, 0)),
            pl.BlockSpec((d, tn), lambda i, j: (0, j)),
            pl.BlockSpec((d, tn), lambda i, j: (0, j + nj)),
            pl.BlockSpec((1, tn), lambda i, j: (0, j)),
            pl.BlockSpec((1, tn), lambda i, j: (0, j + nj)),
        ],
        out_specs=pl.BlockSpec((tm, tn), lambda i, j: (i, j)),
        out_shape=jax.ShapeDtypeStruct((n, dv), _F32),
        compiler_params=_cparams("parallel", "arbitrary"),
        name="pw1_glu",
    )(xb, w, w, bias, bias)


def _convmod_kernel(v_ref, buf0_ref, w_ref, b_ref, g_ref, beta_ref, y_ref, nbuf_ref, vp_scr,
                    c_scr, *, tc, ksz, hist):
    t = pl.program_id(1)
    d = v_ref.shape[-1]
    base = hist - (ksz - 1)

    @pl.when(t == 0)
    def _():
        vp_scr[0:hist, :] = buf0_ref[0]

    vp_scr[hist:hist + tc, :] = v_ref[0]
    for c in range(d // _LANES):
        sl = slice(c * _LANES, (c + 1) * _LANES)
        acc = jnp.broadcast_to(b_ref[:, sl], (tc, _LANES))
        for j in range(_SUBLANES):
            taps = [k for k in range(ksz) if (base + k) % _SUBLANES == j]
            if not taps:
                continue
            qmax = max((base + k) // _SUBLANES for k in taps)
            win = vp_scr[j:j + _SUBLANES * qmax + tc, sl]
            for k in taps:
                q0 = ((base + k) // _SUBLANES) * _SUBLANES
                acc = acc + w_ref[k:k + 1, sl] * win[q0:q0 + tc]
        c_scr[:, sl] = acc
    z = _layer_norm(c_scr[...], g_ref[...], beta_ref[...])
    y_ref[0] = (z * _sigmoid(z)).astype(y_ref.dtype)
    tail = vp_scr[tc:tc + hist, :]
    nbuf_ref[0] = tail
    vp_scr[0:hist, :] = tail


def _convmod(v, buf0, dw_w, dw_b, ln_g, ln_b):
    bsz, t, d = v.shape
    ksz = dw_w.shape[0]
    hist = -(-(ksz - 1) // _SUBLANES) * _SUBLANES
    assert t % _SUBLANES == 0 and d % _LANES == 0
    tc = _pick(t, 128)
    buf_pad = jnp.pad(buf0, ((0, 0), (hist - (ksz - 1), 0), (0, 0)))
    row = lambda a: a.reshape(1, d)
    seq = pl.BlockSpec((1, tc, d), lambda b, i: (b, i, 0))
    per_b = pl.BlockSpec((1, hist, d), lambda b, i: (b, 0, 0))
    y, nbuf = pl.pallas_call(
        functools.partial(_convmod_kernel, tc=tc, ksz=ksz, hist=hist),
        grid=(bsz, t // tc),
        in_specs=[seq, per_b, _resident((ksz, d)), _resident((1, d)), _resident((1, d)),
                  _resident((1, d))],
        out_specs=[seq, per_b],
        out_shape=[jax.ShapeDtypeStruct((bsz, t, d), _BF),
                   jax.ShapeDtypeStruct((bsz, hist, d), _F32)],
        scratch_shapes=[pltpu.VMEM((hist + tc, d), _F32), pltpu.VMEM((tc, d), _F32)],
        compiler_params=_cparams("parallel", "arbitrary"),
        name="convmod",
    )(v, buf_pad, dw_w, row(dw_b), row(ln_g), row(ln_b))
    return y, nbuf[:, hist - (ksz - 1):]


def _router_kernel(x_ref, whi_ref, wlo_ref, gates_ref, *, n_experts):
    x = x_ref[...]
    xhi = x.astype(_BF)
    xlo = (x - xhi.astype(_F32)).astype(_BF)
    whi = whi_ref[...]
    logits = (jnp.dot(xhi, whi, preferred_element_type=_F32)
              + jnp.dot(xlo, whi, preferred_element_type=_F32)
              + jnp.dot(xhi, wlo_ref[...], preferred_element_type=_F32))
    lane = lax.broadcasted_iota(jnp.int32, logits.shape, 1)
    valid = lane < n_experts
    logits = jnp.where(valid, logits, -1e30)
    ex = jnp.where(valid, jnp.exp(logits - jnp.max(logits, axis=1, keepdims=True)), 0.0)
    probs = ex / jnp.sum(ex, axis=1, keepdims=True)
    rest = jnp.where(valid, probs, -1.0)
    gates = jnp.zeros_like(probs)
    total = jnp.zeros_like(probs[:, :1])
    for _ in range(TOP_K):
        top = jnp.max(rest, axis=1, keepdims=True)
        idx = jnp.min(jnp.where(rest == top, lane, _LANES), axis=1, keepdims=True)
        hit = lane == idx
        gates = jnp.where(hit, top, gates)
        total = total + top
        rest = jnp.where(hit, -1.0, rest)
    gates_ref[...] = gates / total


def _router(x, w_router):
    n, d = x.shape
    ne = w_router.shape[1]
    assert ne <= _LANES
    wpad = jnp.pad(w_router, ((0, 0), (0, _LANES - ne)))
    whi = wpad.astype(_BF)
    wlo = (wpad - whi.astype(_F32)).astype(_BF)
    tm = _pick(n, 512)
    return pl.pallas_call(
        functools.partial(_router_kernel, n_experts=ne),
        grid=(n // tm,),
        in_specs=[pl.BlockSpec((tm, d), lambda i: (i, 0)), _resident((d, _LANES)),
                  _resident((d, _LANES))],
        out_specs=pl.BlockSpec((tm, _LANES), lambda i: (i, 0)),
        out_shape=jax.ShapeDtypeStruct((n, _LANES), _F32),
        compiler_params=_cparams("parallel"),
        name="router",
    )(x, whi, wlo)


def _trunk(x, p, lru_conv, lru_h, cm_conv, w):
    bsz, t, d = x.shape
    depth = p.shape[0]
    n = bsz * t
    alpha = (2.0 * depth) ** 0.25
    n_mixers = 2
    x = x.reshape(n, d)
    xb = x
    new_lru_conv, new_lru_h, new_cm_conv = [], [], []
    for i in range(depth):
        j = i // n_mixers
        if i % n_mixers == 0:
            lw = w['rglru_w_out'].shape[1]
            gate, rec = _in_proj(xb, w['rglru_w_in'][j])
            y, buf, hl = _rglru(rec.reshape(bsz, t, lw), gate.reshape(bsz, t, lw), lru_conv[j],
                                lru_h[j], w['rglru_conv_w'][j], w['rglru_conv_b'][j],
                                w['rglru_w_a'][j], w['rglru_b_a'][j], w['rglru_w_x'][j],
                                w['rglru_b_x'][j], w['rglru_lambda'][j])
            new_lru_conv.append(buf)
            new_lru_h.append(hl)
            x, xb = _proj_ln(y.reshape(n, lw), w['rglru_w_out'][j], jnp.zeros((d,), _F32), x,
                             w['ln_mix_g'][i], w['ln_mix_b'][i], alpha)
        else:
            v = _pw1_glu(xb.astype(_BF), w['cm_w_pw1'][j], w['cm_b_pw1'][j])
            y, buf = _convmod(v.reshape(bsz, t, d), cm_conv[j], w['cm_dw_w'][j], w['cm_dw_b'][j],
                              w['cm_ln_g'][j], w['cm_ln_b'][j])
            new_cm_conv.append(buf)
            x, xb = _proj_ln(y.reshape(n, d), w['cm_w_pw2'][j], w['cm_b_pw2'][j], x,
                             w['ln_mix_g'][i], w['ln_mix_b'][i], alpha)
        k = i // 2
        if i % 2 == 0:
            ff = _ffn(xb, w['ffn_w_in'][k][None], w['ffn_w_out'][k][None])
        else:
            gates = _router(x, w['moe_w_router'][k])
            ff = _ffn(xb, w['moe_w_in'][k], w['moe_w_out'][k], gates, tf_pref=1408)
        x, xb = _post(x, ff, w['ln_ffn_g'][i], w['ln_ffn_b'][i], p[i].reshape(n, -1),
                      w['ple_w_gate'][i], w['ple_b_gate'][i], w['ple_w_proj'][i], alpha)
    return (x.reshape(bsz, t, d), jnp.stack(new_lru_conv), jnp.stack(new_lru_h),
            jnp.stack(new_cm_conv))


_MATMUL_WEIGHTS = ('rglru_w_in', 'rglru_w_a', 'rglru_w_x', 'rglru_w_out', 'cm_w_pw1', 'cm_w_pw2',
                   'ffn_w_in', 'ffn_w_out', 'moe_w_in', 'moe_w_out', 'ple_w_proj', 'ple_w_gate')


def kernel(x_prompt, x_sample, p_prompt, p_sample, state_rglru_conv, state_rglru_h, state_conv_module, rglru_w_in, rglru_conv_w, rglru_conv_b, rglru_w_a, rglru_b_a, rglru_w_x, rglru_b_x, rglru_lambda, rglru_w_out, cm_w_pw1, cm_b_pw1, cm_dw_w, cm_dw_b, cm_ln_g, cm_ln_b, cm_w_pw2, cm_b_pw2, ffn_w_in, ffn_w_out, moe_w_router, moe_w_in, moe_w_out, ln_mix_g, ln_mix_b, ln_ffn_g, ln_ffn_b, ple_w_proj, ple_w_gate, ple_b_gate):
    w = dict(rglru_w_in=rglru_w_in, rglru_conv_w=rglru_conv_w, rglru_conv_b=rglru_conv_b,
             rglru_w_a=rglru_w_a, rglru_b_a=rglru_b_a, rglru_w_x=rglru_w_x, rglru_b_x=rglru_b_x,
             rglru_lambda=rglru_lambda, rglru_w_out=rglru_w_out, cm_w_pw1=cm_w_pw1,
             cm_b_pw1=cm_b_pw1, cm_dw_w=cm_dw_w, cm_dw_b=cm_dw_b, cm_ln_g=cm_ln_g,
             cm_ln_b=cm_ln_b, cm_w_pw2=cm_w_pw2, cm_b_pw2=cm_b_pw2, ffn_w_in=ffn_w_in,
             ffn_w_out=ffn_w_out, moe_w_router=moe_w_router, moe_w_in=moe_w_in,
             moe_w_out=moe_w_out, ln_mix_g=ln_mix_g, ln_mix_b=ln_mix_b, ln_ffn_g=ln_ffn_g,
             ln_ffn_b=ln_ffn_b, ple_w_proj=ple_w_proj, ple_w_gate=ple_w_gate,
             ple_b_gate=ple_b_gate)
    for name in _MATMUL_WEIGHTS:
        w[name] = w[name].astype(_BF)
    bp = x_prompt.shape[0]
    dt = x_prompt.dtype
    zero_conv = jnp.zeros((state_rglru_conv.shape[0], bp) + state_rglru_conv.shape[2:], dt)
    zero_h = jnp.zeros((state_rglru_h.shape[0], bp) + state_rglru_h.shape[2:], dt)
    zero_cm = jnp.zeros((state_conv_module.shape[0], bp) + state_conv_module.shape[2:], dt)
    y_p, conv_p, h_p, cm_p = _trunk(x_prompt, p_prompt, zero_conv, zero_h, zero_cm, w)
    y_s, conv_s, h_s, cm_s = _trunk(x_sample, p_sample, state_rglru_conv, state_rglru_h,
                                    state_conv_module, w)
    return (y_p, y_s, conv_p, h_p, cm_p, conv_s, h_s, cm_s)
```

```python
import functools

import jax
import jax.numpy as jnp
from jax import lax
from jax.experimental import pallas as pl
from jax.experimental.pallas import tpu as pltpu

_BF = jnp.bfloat16
_F32 = jnp.float32

LRU_C = 8.0
LN_EPS = 1e-5
TOP_K = 2

_V7X_VMEM_BYTES = 64 * 1024 * 1024
_VMEM_LIMIT = _V7X_VMEM_BYTES - 8 * 1024 * 1024
_VMEM_LIMIT_LARGE = _V7X_VMEM_BYTES - 4 * 1024 * 1024
_SUBLANES = 8
_LANES = 128


def _cparams(*sem):
    return pltpu.CompilerParams(dimension_semantics=sem, vmem_limit_bytes=_VMEM_LIMIT)


def _pick(n, pref, mult=_SUBLANES):
    best = None
    for c in range(mult, min(n, pref) + 1, mult):
        if n % c == 0:
            best = c
    assert best is not None, (n, pref, mult)
    return best


def _resident(shape):
    nd = len(shape)
    return pl.BlockSpec(shape, lambda *_: (0,) * nd, pipeline_mode=pl.Buffered(1))


def _layer_norm(x, g, b):
    mu = jnp.mean(x, axis=-1, keepdims=True)
    xc = x - mu
    var = jnp.mean(xc * xc, axis=-1, keepdims=True)
    return xc * lax.rsqrt(var + LN_EPS) * g + b


def _sigmoid(x):
    return 1.0 / (1.0 + jnp.exp(-x))


def _gelu_tanh(x):
    c = 0.7978845608028654
    return 0.5 * x * (1.0 + jnp.tanh(c * (x + 0.044715 * (x * x * x))))


def _softplus(x):
    return jnp.maximum(x, 0.0) + jnp.log1p(jnp.exp(-jnp.abs(x)))


def _in_proj_kernel(x_ref, wg_ref, wr_ref, gate_ref, rec_ref, xb_scr):
    @pl.when(pl.program_id(1) == 0)
    def _():
        xb_scr[...] = x_ref[...].astype(_BF)

    xb = xb_scr[...]
    gate_ref[...] = _gelu_tanh(jnp.dot(xb, wg_ref[...], preferred_element_type=_F32))
    rec_ref[...] = jnp.dot(xb, wr_ref[...], preferred_element_type=_F32)


def _in_proj(x, w):
    n, d = x.shape
    lw = w.shape[1] // 2
    tm = _pick(n, 1024)
    tn = _pick(lw, 512, _LANES)
    nj = lw // tn
    return pl.pallas_call(
        _in_proj_kernel,
        grid=(n // tm, nj),
        in_specs=[
            pl.BlockSpec((tm, d), lambda i, j: (i, 0)),
            pl.BlockSpec((d, tn), lambda i, j: (0, j)),
            pl.BlockSpec((d, tn), lambda i, j: (0, j + nj)),
        ],
        out_specs=[
            pl.BlockSpec((tm, tn), lambda i, j: (i, j)),
            pl.BlockSpec((tm, tn), lambda i, j: (i, j)),
        ],
        out_shape=[jax.ShapeDtypeStruct((n, lw), _F32)] * 2,
        scratch_shapes=[pltpu.VMEM((tm, d), _BF)],
        compiler_params=_cparams("parallel", "arbitrary"),
        name="in_proj",
    )(x, w, w)


def _rglru_kernel(rec_ref, gate_ref, buf0_ref, h0_ref, cw_ref, cb_ref, wa_ref, ba_ref, wx_ref,
                  bx_ref, lam_ref, y_ref, nbuf_ref, hlast_ref, xp_scr, a_scr, b_scr, carry_scr,
                  *, tc, heads, kw):
    t = pl.program_id(1)
    lw = rec_ref.shape[-1]
    blk = lw // heads
    hist = _SUBLANES
    base = hist - (kw - 1)

    @pl.when(t == 0)
    def _():
        xp_scr[0:hist, :] = buf0_ref[0]
        carry_scr[...] = jnp.broadcast_to(h0_ref[0], (_SUBLANES, lw))

    xp_scr[hist:hist + tc, :] = rec_ref[0]
    decay = -LRU_C * _softplus(-lam_ref[...])
    for h in range(heads):
        sl = slice(h * blk, (h + 1) * blk)
        xc = cb_ref[:, sl] + cw_ref[0:1, sl] * xp_scr[base:base + tc, sl]
        for k in range(1, kw):
            xc = xc + cw_ref[k:k + 1, sl] * xp_scr[base + k:base + k + tc, sl]
        xcb = xc.astype(_BF)
        r = _sigmoid(jnp.dot(xcb, wa_ref[h], preferred_element_type=_F32) + ba_ref[:, sl])
        ig = _sigmoid(jnp.dot(xcb, wx_ref[h], preferred_element_type=_F32) + bx_ref[:, sl])
        log_a = r * decay[:, sl]
        a = jnp.exp(log_a)
        a_scr[:, sl] = a
        b_scr[:, sl] = jnp.sqrt(-jnp.tanh(log_a) * (a * a + 1.0)) * (ig * xc)

    row = lax.broadcasted_iota(jnp.int32, (_SUBLANES, lw), 0)

    def body(g, carry):
        r0 = pl.multiple_of(g * _SUBLANES, _SUBLANES)
        a = a_scr[pl.ds(r0, _SUBLANES), :]
        b = b_scr[pl.ds(r0, _SUBLANES), :]
        for s in (1, 2, 4):
            m = row >= s
            a_sh = jnp.where(m, pltpu.roll(a, s, 0), 1.0)
            b_sh = jnp.where(m, pltpu.roll(b, s, 0), 0.0)
            b = a * b_sh + b
            a = a * a_sh
        hrows = a * carry + b
        b_scr[pl.ds(r0, _SUBLANES), :] = hrows
        return jnp.broadcast_to(hrows[_SUBLANES - 1:_SUBLANES, :], (_SUBLANES, lw))

    carry = lax.fori_loop(0, tc // _SUBLANES, body, carry_scr[...])
    carry_scr[...] = carry
    y_ref[0] = (b_scr[...] * gate_ref[0]).astype(y_ref.dtype)
    hlast_ref[0] = carry
    tail = xp_scr[tc:tc + hist, :]
    nbuf_ref[0] = tail
    xp_scr[0:hist, :] = tail


def _rglru(rec, gate, buf0, h0, conv_w, conv_b, w_a, b_a, w_x, b_x, lam):
    bsz, t, lw = rec.shape
    heads, blk, _ = w_a.shape
    kw = conv_w.shape[0]
    assert kw - 1 <= _SUBLANES <= t and t % _SUBLANES == 0
    tc = _pick(t, 256)
    buf_pad = jnp.pad(buf0, ((0, 0), (_SUBLANES - (kw - 1), 0), (0, 0)))
    row = lambda v: v.reshape(1, lw)
    seq = pl.BlockSpec((1, tc, lw), lambda b, i: (b, i, 0))
    per_b = pl.BlockSpec((1, _SUBLANES, lw), lambda b, i: (b, 0, 0))
    y, nbuf, hlast = pl.pallas_call(
        functools.partial(_rglru_kernel, tc=tc, heads=heads, kw=kw),
        grid=(bsz, t // tc),
        in_specs=[
            seq, seq, per_b,
            pl.BlockSpec((1, 1, lw), lambda b, i: (b, 0, 0)),
            _resident((kw, lw)), _resident((1, lw)),
            _resident((heads, blk, blk)), _resident((1, lw)),
            _resident((heads, blk, blk)), _resident((1, lw)),
            _resident((1, lw)),
        ],
        out_specs=[seq, per_b, per_b],
        out_shape=[
            jax.ShapeDtypeStruct((bsz, t, lw), _BF),
            jax.ShapeDtypeStruct((bsz, _SUBLANES, lw), _F32),
            jax.ShapeDtypeStruct((bsz, _SUBLANES, lw), _F32),
        ],
        scratch_shapes=[
            pltpu.VMEM((_SUBLANES + tc, lw), _F32),
            pltpu.VMEM((tc, lw), _F32),
            pltpu.VMEM((tc, lw), _F32),
            pltpu.VMEM((_SUBLANES, lw), _F32),
        ],
        compiler_params=_cparams("parallel", "arbitrary"),
        name="rglru",
    )(rec, gate, buf_pad, h0.reshape(bsz, 1, lw), conv_w, row(conv_b), w_a, row(b_a), w_x,
      row(b_x), row(lam))
    return y, nbuf[:, _SUBLANES - (kw - 1):], hlast[:, 0]


def _proj_ln_kernel(a_ref, w_ref, bias_ref, x_ref, g_ref, b_ref, out_ref, outb_ref, *, alpha):
    mix = jnp.dot(a_ref[...], w_ref[...], preferred_element_type=_F32) + bias_ref[...]
    y = _layer_norm(alpha * x_ref[...] + mix, g_ref[...], b_ref[...])
    out_ref[...] = y
    outb_ref[...] = y.astype(_BF)


def _proj_ln(a, w, bias, x, g, b, alpha):
    n, k = a.shape
    d = w.shape[1]
    tm = _pick(n, 512)
    tile = lambda width: pl.BlockSpec((tm, width), lambda i: (i, 0))
    return pl.pallas_call(
        functools.partial(_proj_ln_kernel, alpha=alpha),
        grid=(n // tm,),
        in_specs=[tile(k), _resident((k, d)), _resident((1, d)), tile(d), _resident((1, d)),
                  _resident((1, d))],
        out_specs=[tile(d), tile(d)],
        out_shape=[jax.ShapeDtypeStruct((n, d), _F32), jax.ShapeDtypeStruct((n, d), _BF)],
        compiler_params=_cparams("parallel"),
        name="proj_ln",
    )(a, w, bias.reshape(1, d), x, g.reshape(1, d), b.reshape(1, d))


def _ffn_kernel(x_ref, wg_ref, wu_ref, wo_ref, *rest, gated):
    if gated:
        gates_ref, out_ref = rest
    else:
        (out_ref,) = rest
    e = pl.program_id(1)
    f = pl.program_id(2)

    @pl.when((e == 0) & (f == 0))
    def _():
        out_ref[...] = jnp.zeros_like(out_ref)

    xb = x_ref[...]
    g = jnp.dot(xb, wg_ref[0], preferred_element_type=_F32)
    u = jnp.dot(xb, wu_ref[0], preferred_element_type=_F32)
    h = (g * _sigmoid(g)) * u
    if gated:
        lane = lax.broadcasted_iota(jnp.int32, gates_ref.shape, 1)
        h = h * jnp.sum(jnp.where(lane == e, gates_ref[...], 0.0), axis=1, keepdims=True)
    out_ref[...] += jnp.dot(h.astype(_BF), wo_ref[0], preferred_element_type=_F32)


def _ffn(xb, w_in, w_out, gates=None, tf_pref=512):
    n, d = xb.shape
    ne, f, _ = w_out.shape
    tm = _pick(n, 512)
    tf = _pick(f, tf_pref, _LANES)
    nf = f // tf
    in_specs = [
        pl.BlockSpec((tm, d), lambda i, e, j: (i, 0)),
        pl.BlockSpec((1, d, tf), lambda i, e, j: (e, 0, j)),
        pl.BlockSpec((1, d, tf), lambda i, e, j: (e, 0, j + nf)),
        pl.BlockSpec((1, tf, d), lambda i, e, j: (e, j, 0)),
    ]
    args = [xb, w_in, w_in, w_out]
    if gates is not None:
        in_specs.append(pl.BlockSpec((tm, _LANES), lambda i, e, j: (i, 0)))
        args.append(gates)
    return pl.pallas_call(
        functools.partial(_ffn_kernel, gated=gates is not None),
        grid=(n // tm, ne, nf),
        in_specs=in_specs,
        out_specs=pl.BlockSpec((tm, d), lambda i, e, j: (i, 0)),
        out_shape=jax.ShapeDtypeStruct((n, d), _F32),
        compiler_params=_cparams("parallel", "arbitrary", "arbitrary"),
        name="ffn",
    )(*args)


def _post_kernel(x_ref, ff_ref, g_ref, b_ref, p_ref, wg_ref, bg_ref, wp_ref, out_ref, outb_ref,
                 *, alpha):
    xn = _layer_norm(alpha * x_ref[...] + ff_ref[...], g_ref[...], b_ref[...])
    gate = _sigmoid(jnp.dot(xn.astype(_BF), wg_ref[...], preferred_element_type=_F32)
                    + bg_ref[...])
    proj = jnp.dot(p_ref[...].astype(_BF), wp_ref[...], preferred_element_type=_F32)
    y = xn + gate * proj
    out_ref[...] = y
    outb_ref[...] = y.astype(_BF)


def _post(x, ff, g, b, p, w_gate, b_gate, w_proj, alpha):
    n, d = x.shape
    pd = p.shape[1]
    tm = _pick(n, 512)
    tile = lambda width: pl.BlockSpec((tm, width), lambda i: (i, 0))
    return pl.pallas_call(
        functools.partial(_post_kernel, alpha=alpha),
        grid=(n // tm,),
        in_specs=[tile(d), tile(d), _resident((1, d)), _resident((1, d)), tile(pd),
                  _resident((d, d)), _resident((1, d)), _resident((pd, d))],
        out_specs=[tile(d), tile(d)],
        out_shape=[jax.ShapeDtypeStruct((n, d), _F32), jax.ShapeDtypeStruct((n, d), _BF)],
        compiler_params=_cparams("parallel"),
        name="post",
    )(x, ff, g.reshape(1, d), b.reshape(1, d), p, w_gate, b_gate.reshape(1, d), w_proj)


def _pw1_glu_kernel(x_ref, wv_ref, wg_ref, bv_ref, bg_ref, v_ref):
    xb = x_ref[...]
    val = jnp.dot(xb, wv_ref[...], preferred_element_type=_F32) + bv_ref[...]
    gate = jnp.dot(xb, wg_ref[...], preferred_element_type=_F32) + bg_ref[...]
    v_ref[...] = val * _sigmoid(gate)


def _pw1_glu(xb, w, bias):
    n, d = xb.shape
    dv = w.shape[1] // 2
    tm = _pick(n, 1024)
    tn = _pick(dv, 512, _LANES)
    nj = dv // tn
    bias = bias.reshape(1, 2 * dv)
    return pl.pallas_call(
        _pw1_glu_kernel,
        grid=(n // tm, nj),
        in_specs=[
            pl.BlockSpec((tm, d), lambda i, j: (i, 0)),
            pl.BlockSpec((d, tn), lambda i, j: (0, j)),
            pl.BlockSpec((d, tn), lambda i, j: (0, j + nj)),
            pl.BlockSpec((1, tn), lambda i, j: (0, j)),
            pl.BlockSpec((1, tn), lambda i, j: (0, j + nj)),
        ],
        out_specs=pl.BlockSpec((tm, tn), lambda i, j: (i, j)),
        out_shape=jax.ShapeDtypeStruct((n, dv), _F32),
        compiler_params=_cparams("parallel", "arbitrary"),
        name="pw1_glu",
    )(xb, w, w, bias, bias)


def _convmod_kernel(v_ref, buf0_ref, w_ref, b_ref, g_ref, beta_ref, y_ref, nbuf_ref, vp_scr,
                    c_scr, *, tc, ksz, hist):
    t = pl.program_id(1)
    d = v_ref.shape[-1]
    base = hist - (ksz - 1)

    @pl.when(t == 0)
    def _():
        vp_scr[0:hist, :] = buf0_ref[0]

    vp_scr[hist:hist + tc, :] = v_ref[0]
    rows = hist + tc
    for c in range(d // _LANES):
        sl = slice(c * _LANES, (c + 1) * _LANES)
        acc = jnp.broadcast_to(b_ref[:, sl], (tc, _LANES))
        whole = vp_scr[:, sl]
        for j in range(_SUBLANES):
            taps = [k for k in range(ksz) if (base + k) % _SUBLANES == j]
            if not taps:
                continue
            win = whole if j == 0 else pltpu.roll(whole, rows - j, 0)
            for k in taps:
                q0 = ((base + k) // _SUBLANES) * _SUBLANES
                acc = acc + w_ref[k:k + 1, sl] * win[q0:q0 + tc]
        c_scr[:, sl] = acc
    z = _layer_norm(c_scr[...], g_ref[...], beta_ref[...])
    y_ref[0] = (z * _sigmoid(z)).astype(y_ref.dtype)
    tail = vp_scr[tc:tc + hist, :]
    nbuf_ref[0] = tail
    vp_scr[0:hist, :] = tail


def _convmod(v, buf0, dw_w, dw_b, ln_g, ln_b):
    bsz, t, d = v.shape
    ksz = dw_w.shape[0]
    hist = -(-(ksz - 1) // _SUBLANES) * _SUBLANES
    assert t % _SUBLANES == 0 and d % _LANES == 0
    tc = _pick(t, 128)
    buf_pad = jnp.pad(buf0, ((0, 0), (hist - (ksz - 1), 0), (0, 0)))
    row = lambda a: a.reshape(1, d)
    seq = pl.BlockSpec((1, tc, d), lambda b, i: (b, i, 0))
    per_b = pl.BlockSpec((1, hist, d), lambda b, i: (b, 0, 0))
    y, nbuf = pl.pallas_call(
        functools.partial(_convmod_kernel, tc=tc, ksz=ksz, hist=hist),
        grid=(bsz, t // tc),
        in_specs=[seq, per_b, _resident((ksz, d)), _resident((1, d)), _resident((1, d)),
                  _resident((1, d))],
        out_specs=[seq, per_b],
        out_shape=[jax.ShapeDtypeStruct((bsz, t, d), _BF),
                   jax.ShapeDtypeStruct((bsz, hist, d), _F32)],
        scratch_shapes=[pltpu.VMEM((hist + tc, d), _F32), pltpu.VMEM((tc, d), _F32)],
        compiler_params=_cparams("parallel", "arbitrary"),
        name="convmod",
    )(v, buf_pad, dw_w, row(dw_b), row(ln_g), row(ln_b))
    return y, nbuf[:, hist - (ksz - 1):]


_META_EXPERT, _META_GATE, _META_RANK = 0, TOP_K, 2 * TOP_K


def _router_kernel(xb_ref, w_ref, gates_ref, meta_ref, counts_ref, run_scr, *, n_experts):
    @pl.when(pl.program_id(0) == 0)
    def _():
        run_scr[...] = jnp.zeros_like(run_scr)

    logits = jnp.dot(xb_ref[...], w_ref[...], preferred_element_type=_F32)
    tm = logits.shape[0]
    lane = lax.broadcasted_iota(jnp.int32, logits.shape, 1)
    valid = lane < n_experts
    logits = jnp.where(valid, logits, -1e30)
    ex = jnp.where(valid, jnp.exp(logits - jnp.max(logits, axis=1, keepdims=True)), 0.0)
    probs = ex / jnp.sum(ex, axis=1, keepdims=True)
    rest = jnp.where(valid, probs, -1.0)
    total = jnp.zeros_like(probs[:, :1])
    picks = []
    for _ in range(TOP_K):
        top = jnp.max(rest, axis=1, keepdims=True)
        idx = jnp.min(jnp.where(rest == top, lane, _LANES), axis=1, keepdims=True)
        hit = lane == idx
        picks.append((top, idx, hit))
        total = total + top
        rest = jnp.where(hit, -1.0, rest)
    gates = jnp.zeros_like(probs)
    chosen = jnp.zeros_like(probs)
    for top, _, hit in picks:
        gates = jnp.where(hit, top / total, gates)
        chosen = jnp.where(hit, 1.0, chosen)
    gates_ref[...] = gates
    r_i = lax.broadcasted_iota(jnp.int32, (tm, tm), 0)
    c_i = lax.broadcasted_iota(jnp.int32, (tm, tm), 1)
    tri = jnp.where(r_i > c_i, 1.0, 0.0).astype(_BF)
    before = jnp.dot(tri, chosen.astype(_BF), preferred_element_type=_F32) + run_scr[...]
    meta = jnp.zeros_like(probs)
    for k, (top, idx, hit) in enumerate(picks):
        rank = jnp.sum(jnp.where(hit, before, 0.0), axis=1, keepdims=True)
        meta = jnp.where(lane == _META_EXPERT + k, idx.astype(_F32), meta)
        meta = jnp.where(lane == _META_GATE + k, top / total, meta)
        meta = jnp.where(lane == _META_RANK + k, rank, meta)
    meta_ref[...] = meta
    run_scr[...] += jnp.sum(chosen, axis=0, keepdims=True)
    counts_ref[...] = run_scr[...]


def _router(xb, w_router):
    n, d = xb.shape
    ne = w_router.shape[1]
    assert ne <= _LANES and 3 * TOP_K <= _LANES
    wpad = jnp.pad(w_router, ((0, 0), (0, _LANES - ne))).astype(_BF)
    tm = _pick(n, 512)
    tile = pl.BlockSpec((tm, _LANES), lambda i: (i, 0))
    return pl.pallas_call(
        functools.partial(_router_kernel, n_experts=ne),
        grid=(n // tm,),
        in_specs=[pl.BlockSpec((tm, d), lambda i: (i, 0)), _resident((d, _LANES))],
        out_specs=[tile, tile, pl.BlockSpec((1, _LANES), lambda i: (0, 0))],
        out_shape=[jax.ShapeDtypeStruct((n, _LANES), _F32), jax.ShapeDtypeStruct((n, _LANES), _F32),
                   jax.ShapeDtypeStruct((1, _LANES), _F32)],
        scratch_shapes=[pltpu.VMEM((1, _LANES), _F32)],
        compiler_params=_cparams("arbitrary"),
        name="router",
    )(xb, wpad)


def _route_plan(meta, counts_row, n_experts, tg):
    n = meta.shape[0]
    expert = meta[:, _META_EXPERT:_META_EXPERT + TOP_K].astype(jnp.int32)
    rank = meta[:, _META_RANK:_META_RANK + TOP_K].astype(jnp.int32)
    counts = counts_row[0, :n_experts].astype(jnp.int32)
    padded = (counts + tg - 1) // tg * tg
    ends = jnp.cumsum(padded)
    starts = ends - padded
    pos = (jnp.take(starts, expert) + rank).reshape(-1)
    n_tiles = -(-(TOP_K * n + n_experts * (tg - 1)) // tg)
    tile_start = jnp.arange(n_tiles, dtype=jnp.int32) * tg
    tile_expert = jnp.sum((tile_start[:, None] >= ends[None, :]).astype(jnp.int32), axis=1)
    tile_expert = jnp.minimum(tile_expert, n_experts - 1)
    n_used = (ends[-1:] // tg).astype(jnp.int32)
    return pos, tile_expert, n_used, n_tiles


def _dispatch_kernel(pos_ref, x_ref, init_hbm, out_hbm, sem, *, tm):
    del init_hbm
    base = pl.program_id(0) * (TOP_K * tm)

    def issue(r, c):
        for k in range(TOP_K):
            dst = pos_ref[base + TOP_K * r + k]
            pltpu.make_async_copy(x_ref.at[pl.ds(r, 1)], out_hbm.at[pl.ds(dst, 1)], sem).start()
        return c

    lax.fori_loop(0, tm, issue, 0)

    def drain(r, c):
        for k in range(TOP_K):
            pltpu.make_async_copy(x_ref.at[pl.ds(0, 1)], out_hbm.at[pl.ds(0, 1)], sem).wait()
        return c

    lax.fori_loop(0, tm, drain, 0)


def _dispatch(x, pos, n_rows):
    n, d = x.shape
    tm = _pick(n, 256)
    return pl.pallas_call(
        functools.partial(_dispatch_kernel, tm=tm),
        grid_spec=pltpu.PrefetchScalarGridSpec(
            num_scalar_prefetch=1,
            grid=(n // tm,),
            in_specs=[pl.BlockSpec((tm, d), lambda i, pos: (i, 0)),
                      pl.BlockSpec(memory_space=pl.ANY)],
            out_specs=pl.BlockSpec(memory_space=pl.ANY),
            scratch_shapes=[pltpu.SemaphoreType.DMA(())],
        ),
        out_shape=jax.ShapeDtypeStruct((n_rows, d), x.dtype),
        input_output_aliases={2: 0},
        compiler_params=_cparams("arbitrary"),
        name="dispatch",
    )(pos, x, jnp.zeros((n_rows, d), x.dtype))


def _grouped_ffn_kernel(te_ref, nu_ref, x_ref, wg_ref, wu_ref, wo_ref, out_ref):
    del te_ref
    i = pl.program_id(0)
    j = pl.program_id(1)
    used = i < nu_ref[0]

    def contrib():
        xb = x_ref[...].astype(_BF)
        g = jnp.dot(xb, wg_ref[0], preferred_element_type=_F32)
        u = jnp.dot(xb, wu_ref[0], preferred_element_type=_F32)
        h = ((g * _sigmoid(g)) * u).astype(_BF)
        return jnp.dot(h, wo_ref[0], preferred_element_type=_F32)

    @pl.when(used & (j == 0))
    def _():
        out_ref[...] = contrib()

    @pl.when(used & (j > 0))
    def _():
        out_ref[...] += contrib()

    @pl.when(jnp.logical_not(used) & (j == 0))
    def _():
        out_ref[...] = jnp.zeros_like(out_ref)


def _grouped_ffn(xs, w_in, w_out, tile_expert, n_used, tg, tf_pref):
    r, d = xs.shape
    f = w_out.shape[1]
    tf = _pick(f, tf_pref, _LANES)
    nf = f // tf

    def wcol(off):
        def index(i, j, te, nu):
            return (te[i], 0, jnp.where(i < nu[0], j, nf - 1) + off)
        return index

    return pl.pallas_call(
        _grouped_ffn_kernel,
        grid_spec=pltpu.PrefetchScalarGridSpec(
            num_scalar_prefetch=2,
            grid=(r // tg, nf),
            in_specs=[
                pl.BlockSpec((tg, d), lambda i, j, te, nu: (i, 0)),
                pl.BlockSpec((1, d, tf), wcol(0)),
                pl.BlockSpec((1, d, tf), wcol(nf)),
                pl.BlockSpec((1, tf, d),
                             lambda i, j, te, nu: (te[i], jnp.where(i < nu[0], j, nf - 1), 0)),
            ],
            out_specs=pl.BlockSpec((tg, d), lambda i, j, te, nu: (i, 0)),
        ),
        out_shape=jax.ShapeDtypeStruct((r, d), _F32),
        compiler_params=pltpu.CompilerParams(dimension_semantics=("parallel", "arbitrary"),
                                             vmem_limit_bytes=_VMEM_LIMIT_LARGE),
        name="grouped_ffn",
    )(tile_expert, n_used, xs, w_in, w_in, w_out)


def _moe_post_kernel(pos_ref, x_ref, meta_ref, ys_hbm, g_ref, b_ref, p_ref, wg_ref, bg_ref, wp_ref,
                     out_ref, outb_ref, buf, sem, *, alpha, tm):
    base = pl.program_id(0) * (TOP_K * tm)

    def issue(r, c):
        for k in range(TOP_K):
            src = pos_ref[base + TOP_K * r + k]
            pltpu.make_async_copy(ys_hbm.at[pl.ds(src, 1)], buf.at[k, pl.ds(r, 1)], sem).start()
        return c

    lax.fori_loop(0, tm, issue, 0)

    def drain(r, c):
        for k in range(TOP_K):
            pltpu.make_async_copy(ys_hbm.at[pl.ds(0, 1)], buf.at[0, pl.ds(0, 1)], sem).wait()
        return c

    lax.fori_loop(0, tm, drain, 0)
    meta = meta_ref[...]
    ff = meta[:, _META_GATE:_META_GATE + 1] * buf[0]
    for k in range(1, TOP_K):
        ff = ff + meta[:, _META_GATE + k:_META_GATE + k + 1] * buf[k]
    xn = _layer_norm(alpha * x_ref[...] + ff, g_ref[...], b_ref[...])
    gate = _sigmoid(jnp.dot(xn.astype(_BF), wg_ref[...], preferred_element_type=_F32)
                    + bg_ref[...])
    proj = jnp.dot(p_ref[...].astype(_BF), wp_ref[...], preferred_element_type=_F32)
    y = xn + gate * proj
    out_ref[...] = y
    outb_ref[...] = y.astype(_BF)


def _moe_post(x, ys, pos, meta, g, b, p, w_gate, b_gate, w_proj, alpha):
    n, d = x.shape
    pd = p.shape[1]
    tm = _pick(n, 256)
    tile = lambda width: pl.BlockSpec((tm, width), lambda i, pos: (i, 0))
    res = lambda shape: pl.BlockSpec(shape, lambda i, pos: (0,) * len(shape),
                                     pipeline_mode=pl.Buffered(1))
    return pl.pallas_call(
        functools.partial(_moe_post_kernel, alpha=alpha, tm=tm),
        grid_spec=pltpu.PrefetchScalarGridSpec(
            num_scalar_prefetch=1,
            grid=(n // tm,),
            in_specs=[tile(d), tile(_LANES), pl.BlockSpec(memory_space=pl.ANY), res((1, d)),
                      res((1, d)), tile(pd), res((d, d)), res((1, d)), res((pd, d))],
            out_specs=[tile(d), tile(d)],
            scratch_shapes=[pltpu.VMEM((TOP_K, tm, d), _F32), pltpu.SemaphoreType.DMA(())],
        ),
        out_shape=[jax.ShapeDtypeStruct((n, d), _F32), jax.ShapeDtypeStruct((n, d), _BF)],
        compiler_params=_cparams("arbitrary"),
        name="moe_post",
    )(pos, x, meta, ys, g.reshape(1, d), b.reshape(1, d), p, w_gate, b_gate.reshape(1, d), w_proj)


_SPARSE_MOE_MIN_TOKENS = 4096
_GROUP_TILE_ROWS = 512


def _trunk(x, p, lru_conv, lru_h, cm_conv, w):
    bsz, t, d = x.shape
    depth = p.shape[0]
    n = bsz * t
    alpha = (2.0 * depth) ** 0.25
    n_mixers = 2
    x = x.reshape(n, d)
    xb = x
    new_lru_conv, new_lru_h, new_cm_conv = [], [], []
    for i in range(depth):
        j = i // n_mixers
        if i % n_mixers == 0:
            lw = w['rglru_w_out'].shape[1]
            gate, rec = _in_proj(xb, w['rglru_w_in'][j])
            y, buf, hl = _rglru(rec.reshape(bsz, t, lw), gate.reshape(bsz, t, lw), lru_conv[j],
                                lru_h[j], w['rglru_conv_w'][j], w['rglru_conv_b'][j],
                                w['rglru_w_a'][j], w['rglru_b_a'][j], w['rglru_w_x'][j],
                                w['rglru_b_x'][j], w['rglru_lambda'][j])
            new_lru_conv.append(buf)
            new_lru_h.append(hl)
            x, xb = _proj_ln(y.reshape(n, lw), w['rglru_w_out'][j], jnp.zeros((d,), _F32), x,
                             w['ln_mix_g'][i], w['ln_mix_b'][i], alpha)
        else:
            v = _pw1_glu(xb.astype(_BF), w['cm_w_pw1'][j], w['cm_b_pw1'][j])
            y, buf = _convmod(v.reshape(bsz, t, d), cm_conv[j], w['cm_dw_w'][j], w['cm_dw_b'][j],
                              w['cm_ln_g'][j], w['cm_ln_b'][j])
            new_cm_conv.append(buf)
            x, xb = _proj_ln(y.reshape(n, d), w['cm_w_pw2'][j], w['cm_b_pw2'][j], x,
                             w['ln_mix_g'][i], w['ln_mix_b'][i], alpha)
        k = i // 2
        post_args = (w['ln_ffn_g'][i], w['ln_ffn_b'][i], p[i].reshape(n, -1), w['ple_w_gate'][i],
                     w['ple_b_gate'][i], w['ple_w_proj'][i], alpha)
        if i % 2 == 0:
            ff = _ffn(xb, w['ffn_w_in'][k][None], w['ffn_w_out'][k][None])
            x, xb = _post(x, ff, *post_args)
        else:
            gates, meta, counts = _router(xb, w['moe_w_router'][k])
            ne = w['moe_w_router'].shape[-1]
            if n < _SPARSE_MOE_MIN_TOKENS:
                ff = _ffn(xb, w['moe_w_in'][k], w['moe_w_out'][k], gates, tf_pref=1408)
                x, xb = _post(x, ff, *post_args)
            else:
                tg = _GROUP_TILE_ROWS
                pos, tile_expert, n_used, n_tiles = _route_plan(meta, counts, ne, tg)
                xs = _dispatch(x, pos, n_tiles * tg)
                ys = _grouped_ffn(xs, w['moe_w_in'][k], w['moe_w_out'][k], tile_expert, n_used,
                                  tg, tf_pref=1408)
                x, xb = _moe_post(x, ys, pos, meta, *post_args)
    return (x.reshape(bsz, t, d), jnp.stack(new_lru_conv), jnp.stack(new_lru_h),
            jnp.stack(new_cm_conv))


_MATMUL_WEIGHTS = ('rglru_w_in', 'rglru_w_a', 'rglru_w_x', 'rglru_w_out', 'cm_w_pw1', 'cm_w_pw2',
                   'ffn_w_in', 'ffn_w_out', 'moe_w_in', 'moe_w_out', 'ple_w_proj', 'ple_w_gate')


def kernel(x_prompt, x_sample, p_prompt, p_sample, state_rglru_conv, state_rglru_h, state_conv_module, rglru_w_in, rglru_conv_w, rglru_conv_b, rglru_w_a, rglru_b_a, rglru_w_x, rglru_b_x, rglru_lambda, rglru_w_out, cm_w_pw1, cm_b_pw1, cm_dw_w, cm_dw_b, cm_ln_g, cm_ln_b, cm_w_pw2, cm_b_pw2, ffn_w_in, ffn_w_out, moe_w_router, moe_w_in, moe_w_out, ln_mix_g, ln_mix_b, ln_ffn_g, ln_ffn_b, ple_w_proj, ple_w_gate, ple_b_gate):
    w = dict(rglru_w_in=rglru_w_in, rglru_conv_w=rglru_conv_w, rglru_conv_b=rglru_conv_b,
             rglru_w_a=rglru_w_a, rglru_b_a=rglru_b_a, rglru_w_x=rglru_w_x, rglru_b_x=rglru_b_x,
             rglru_lambda=rglru_lambda, rglru_w_out=rglru_w_out, cm_w_pw1=cm_w_pw1,
             cm_b_pw1=cm_b_pw1, cm_dw_w=cm_dw_w, cm_dw_b=cm_dw_b, cm_ln_g=cm_ln_g,
             cm_ln_b=cm_ln_b, cm_w_pw2=cm_w_pw2, cm_b_pw2=cm_b_pw2, ffn_w_in=ffn_w_in,
             ffn_w_out=ffn_w_out, moe_w_router=moe_w_router, moe_w_in=moe_w_in,
             moe_w_out=moe_w_out, ln_mix_g=ln_mix_g, ln_mix_b=ln_mix_b, ln_ffn_g=ln_ffn_g,
             ln_ffn_b=ln_ffn_b, ple_w_proj=ple_w_proj, ple_w_gate=ple_w_gate,
             ple_b_gate=ple_b_gate)
    for name in _MATMUL_WEIGHTS:
        w[name] = w[name].astype(_BF)
    bp = x_prompt.shape[0]
    dt = x_prompt.dtype
    zero_conv = jnp.zeros((state_rglru_conv.shape[0], bp) + state_rglru_conv.shape[2:], dt)
    zero_h = jnp.zeros((state_rglru_h.shape[0], bp) + state_rglru_h.shape[2:], dt)
    zero_cm = jnp.zeros((state_conv_module.shape[0], bp) + state_conv_module.shape[2:], dt)
    y_p, conv_p, h_p, cm_p = _trunk(x_prompt, p_prompt, zero_conv, zero_h, zero_cm, w)
    y_s, conv_s, h_s, cm_s = _trunk(x_sample, p_sample, state_rglru_conv, state_rglru_h,
                                    state_conv_module, w)
    return (y_p, y_s, conv_p, h_p, cm_p, conv_s, h_s, cm_s)
```

```python
import functools

import jax
import jax.numpy as jnp
from jax import lax
from jax.experimental import pallas as pl
from jax.experimental.pallas import tpu as pltpu

_BF = jnp.bfloat16
_F32 = jnp.float32

LRU_C = 8.0
LN_EPS = 1e-5
TOP_K = 2

_V7X_VMEM_BYTES = 64 * 1024 * 1024
_VMEM_LIMIT = _V7X_VMEM_BYTES - 8 * 1024 * 1024
_VMEM_LIMIT_LARGE = _V7X_VMEM_BYTES - 4 * 1024 * 1024
_SUBLANES = 8
_LANES = 128
_V7X_MXU_DIM = 256
_ROW_BLOCK = 128


def _cparams(*sem):
    return pltpu.CompilerParams(dimension_semantics=sem, vmem_limit_bytes=_VMEM_LIMIT)


def _pick(n, pref, mult=_SUBLANES):
    best = None
    for c in range(mult, min(n, pref) + 1, mult):
        if n % c == 0:
            best = c
    assert best is not None, (n, pref, mult)
    return best


def _row_blocks(tm):
    rb = _pick(tm, _ROW_BLOCK)
    return [slice(r0, r0 + rb) for r0 in range(0, tm, rb)]


def _for_each_row(tm, issue):
    def group(g, c):
        r0 = pl.multiple_of(g * _SUBLANES, _SUBLANES)
        for j in range(_SUBLANES):
            for k in range(TOP_K):
                issue(r0 + j, k)
        return c

    lax.fori_loop(0, tm // _SUBLANES, group, 0)


def _resident(shape):
    nd = len(shape)
    return pl.BlockSpec(shape, lambda *_: (0,) * nd, pipeline_mode=pl.Buffered(1))


def _layer_norm(x, g, b):
    mu = jnp.mean(x, axis=-1, keepdims=True)
    xc = x - mu
    var = jnp.mean(xc * xc, axis=-1, keepdims=True)
    return xc * lax.rsqrt(var + LN_EPS) * g + b


def _sigmoid(x):
    return 1.0 / (1.0 + jnp.exp(-x))


def _gelu_tanh(x):
    c = 0.7978845608028654
    return 0.5 * x * (1.0 + jnp.tanh(c * (x + 0.044715 * (x * x * x))))


def _softplus(x):
    return jnp.maximum(x, 0.0) + jnp.log1p(jnp.exp(-jnp.abs(x)))


def _in_proj_kernel(x_ref, wg_ref, wr_ref, gate_ref, rec_ref, xb_scr):
    @pl.when(pl.program_id(1) == 0)
    def _():
        xb_scr[...] = x_ref[...].astype(_BF)

    for rows in _row_blocks(x_ref.shape[0]):
        xb = xb_scr[rows, :]
        gate_ref[rows, :] = _gelu_tanh(jnp.dot(xb, wg_ref[...], preferred_element_type=_F32))
        rec_ref[rows, :] = jnp.dot(xb, wr_ref[...], preferred_element_type=_F32)


def _in_proj(x, w):
    n, d = x.shape
    lw = w.shape[1] // 2
    tm = _pick(n, 1024)
    tn = _pick(lw, 512, _LANES)
    nj = lw // tn
    return pl.pallas_call(
        _in_proj_kernel,
        grid=(n // tm, nj),
        in_specs=[
            pl.BlockSpec((tm, d), lambda i, j: (i, 0)),
            pl.BlockSpec((d, tn), lambda i, j: (0, j)),
            pl.BlockSpec((d, tn), lambda i, j: (0, j + nj)),
        ],
        out_specs=[
            pl.BlockSpec((tm, tn), lambda i, j: (i, j)),
            pl.BlockSpec((tm, tn), lambda i, j: (i, j)),
        ],
        out_shape=[jax.ShapeDtypeStruct((n, lw), _F32)] * 2,
        scratch_shapes=[pltpu.VMEM((tm, d), _BF)],
        compiler_params=_cparams("parallel", "arbitrary"),
        name="in_proj",
    )(x, w, w)


def _rglru_kernel(rec_ref, gate_ref, buf0_ref, h0_ref, cw_ref, cb_ref, wa_ref, ba_ref, wx_ref,
                  bx_ref, lam_ref, y_ref, nbuf_ref, hlast_ref, xp_scr, a_scr, b_scr, carry_scr,
                  *, tc, heads, kw):
    t = pl.program_id(1)
    lw = rec_ref.shape[-1]
    blk = lw // heads
    hist = _SUBLANES
    base = hist - (kw - 1)

    @pl.when(t == 0)
    def _():
        xp_scr[0:hist, :] = buf0_ref[0]
        carry_scr[...] = jnp.broadcast_to(h0_ref[0], (_SUBLANES, lw))

    xp_scr[hist:hist + tc, :] = rec_ref[0]
    decay = -LRU_C * _softplus(-lam_ref[...])
    for h in range(heads):
        sl = slice(h * blk, (h + 1) * blk)
        xc = cb_ref[:, sl] + cw_ref[0:1, sl] * xp_scr[base:base + tc, sl]
        for k in range(1, kw):
            xc = xc + cw_ref[k:k + 1, sl] * xp_scr[base + k:base + k + tc, sl]
        xcb = xc.astype(_BF)
        r = _sigmoid(jnp.dot(xcb, wa_ref[h], preferred_element_type=_F32) + ba_ref[:, sl])
        ig = _sigmoid(jnp.dot(xcb, wx_ref[h], preferred_element_type=_F32) + bx_ref[:, sl])
        log_a = r * decay[:, sl]
        a = jnp.exp(log_a)
        a_scr[:, sl] = a
        b_scr[:, sl] = jnp.sqrt(-jnp.tanh(log_a) * (a * a + 1.0)) * (ig * xc)

    row = lax.broadcasted_iota(jnp.int32, (_SUBLANES, lw), 0)

    def body(g, carry):
        r0 = pl.multiple_of(g * _SUBLANES, _SUBLANES)
        a = a_scr[pl.ds(r0, _SUBLANES), :]
        b = b_scr[pl.ds(r0, _SUBLANES), :]
        for s in (1, 2, 4):
            m = row >= s
            a_sh = jnp.where(m, pltpu.roll(a, s, 0), 1.0)
            b_sh = jnp.where(m, pltpu.roll(b, s, 0), 0.0)
            b = a * b_sh + b
            a = a * a_sh
        hrows = a * carry + b
        b_scr[pl.ds(r0, _SUBLANES), :] = hrows
        return jnp.broadcast_to(hrows[_SUBLANES - 1:_SUBLANES, :], (_SUBLANES, lw))

    carry = lax.fori_loop(0, tc // _SUBLANES, body, carry_scr[...])
    carry_scr[...] = carry
    y_ref[0] = (b_scr[...] * gate_ref[0]).astype(y_ref.dtype)
    hlast_ref[0] = carry
    tail = xp_scr[tc:tc + hist, :]
    nbuf_ref[0] = tail
    xp_scr[0:hist, :] = tail


def _rglru(rec, gate, buf0, h0, conv_w, conv_b, w_a, b_a, w_x, b_x, lam):
    bsz, t, lw = rec.shape
    heads, blk, _ = w_a.shape
    kw = conv_w.shape[0]
    assert kw - 1 <= _SUBLANES <= t and t % _SUBLANES == 0
    tc = _pick(t, 256)
    buf_pad = jnp.pad(buf0, ((0, 0), (_SUBLANES - (kw - 1), 0), (0, 0)))
    row = lambda v: v.reshape(1, lw)
    seq = pl.BlockSpec((1, tc, lw), lambda b, i: (b, i, 0))
    per_b = pl.BlockSpec((1, _SUBLANES, lw), lambda b, i: (b, 0, 0))
    y, nbuf, hlast = pl.pallas_call(
        functools.partial(_rglru_kernel, tc=tc, heads=heads, kw=kw),
        grid=(bsz, t // tc),
        in_specs=[
            seq, seq, per_b,
            pl.BlockSpec((1, 1, lw), lambda b, i: (b, 0, 0)),
            _resident((kw, lw)), _resident((1, lw)),
            _resident((heads, blk, blk)), _resident((1, lw)),
            _resident((heads, blk, blk)), _resident((1, lw)),
            _resident((1, lw)),
        ],
        out_specs=[seq, per_b, per_b],
        out_shape=[
            jax.ShapeDtypeStruct((bsz, t, lw), _BF),
            jax.ShapeDtypeStruct((bsz, _SUBLANES, lw), _F32),
            jax.ShapeDtypeStruct((bsz, _SUBLANES, lw), _F32),
        ],
        scratch_shapes=[
            pltpu.VMEM((_SUBLANES + tc, lw), _F32),
            pltpu.VMEM((tc, lw), _F32),
            pltpu.VMEM((tc, lw), _F32),
            pltpu.VMEM((_SUBLANES, lw), _F32),
        ],
        compiler_params=_cparams("parallel", "arbitrary"),
        name="rglru",
    )(rec, gate, buf_pad, h0.reshape(bsz, 1, lw), conv_w, row(conv_b), w_a, row(b_a), w_x,
      row(b_x), row(lam))
    return y, nbuf[:, _SUBLANES - (kw - 1):], hlast[:, 0]


def _proj_ln_kernel(a_ref, w_ref, bias_ref, x_ref, g_ref, b_ref, out_ref, outb_ref, *, alpha):
    for rows in _row_blocks(a_ref.shape[0]):
        mix = jnp.dot(a_ref[rows, :], w_ref[...], preferred_element_type=_F32) + bias_ref[...]
        y = _layer_norm(alpha * x_ref[rows, :] + mix, g_ref[...], b_ref[...])
        out_ref[rows, :] = y
        outb_ref[rows, :] = y.astype(_BF)


def _proj_ln(a, w, bias, x, g, b, alpha):
    n, k = a.shape
    d = w.shape[1]
    tm = _pick(n, 512)
    tile = lambda width: pl.BlockSpec((tm, width), lambda i: (i, 0))
    return pl.pallas_call(
        functools.partial(_proj_ln_kernel, alpha=alpha),
        grid=(n // tm,),
        in_specs=[tile(k), _resident((k, d)), _resident((1, d)), tile(d), _resident((1, d)),
                  _resident((1, d))],
        out_specs=[tile(d), tile(d)],
        out_shape=[jax.ShapeDtypeStruct((n, d), _F32), jax.ShapeDtypeStruct((n, d), _BF)],
        compiler_params=_cparams("parallel"),
        name="proj_ln",
    )(a, w, bias.reshape(1, d), x, g.reshape(1, d), b.reshape(1, d))


def _ffn_kernel(x_ref, wg_ref, wu_ref, wo_ref, *rest, gated):
    if gated:
        gates_ref, out_ref = rest
    else:
        (out_ref,) = rest
    e = pl.program_id(1)
    f = pl.program_id(2)

    @pl.when((e == 0) & (f == 0))
    def _():
        out_ref[...] = jnp.zeros_like(out_ref)

    xb = x_ref[...]
    g = jnp.dot(xb, wg_ref[0], preferred_element_type=_F32)
    u = jnp.dot(xb, wu_ref[0], preferred_element_type=_F32)
    h = (g * _sigmoid(g)) * u
    if gated:
        lane = lax.broadcasted_iota(jnp.int32, gates_ref.shape, 1)
        h = h * jnp.sum(jnp.where(lane == e, gates_ref[...], 0.0), axis=1, keepdims=True)
    out_ref[...] += jnp.dot(h.astype(_BF), wo_ref[0], preferred_element_type=_F32)


def _ffn(xb, w_in, w_out, gates=None, tf_pref=512):
    n, d = xb.shape
    ne, f, _ = w_out.shape
    tm = _pick(n, 1024)
    tf = _pick(f, tf_pref, _LANES)
    nf = f // tf
    in_specs = [
        pl.BlockSpec((tm, d), lambda i, e, j: (i, 0)),
        pl.BlockSpec((1, d, tf), lambda i, e, j: (e, 0, j)),
        pl.BlockSpec((1, d, tf), lambda i, e, j: (e, 0, j + nf)),
        pl.BlockSpec((1, tf, d), lambda i, e, j: (e, j, 0)),
    ]
    args = [xb, w_in, w_in, w_out]
    if gates is not None:
        in_specs.append(pl.BlockSpec((tm, _LANES), lambda i, e, j: (i, 0)))
        args.append(gates)
    return pl.pallas_call(
        functools.partial(_ffn_kernel, gated=gates is not None),
        grid=(n // tm, ne, nf),
        in_specs=in_specs,
        out_specs=pl.BlockSpec((tm, d), lambda i, e, j: (i, 0)),
        out_shape=jax.ShapeDtypeStruct((n, d), _F32),
        compiler_params=_cparams("parallel", "arbitrary", "arbitrary"),
        name="ffn",
    )(*args)


def _norm_ple(pre, p, g_ref, b_ref, wg_ref, bg_ref, wp_ref):
    xn = _layer_norm(pre, g_ref[...], b_ref[...])
    gate = _sigmoid(jnp.dot(xn.astype(_BF), wg_ref[...], preferred_element_type=_F32)
                    + bg_ref[...])
    proj = jnp.dot(p.astype(_BF), wp_ref[...], preferred_element_type=_F32)
    return xn + gate * proj


def _post_kernel(x_ref, ff_ref, g_ref, b_ref, p_ref, wg_ref, bg_ref, wp_ref, out_ref, outb_ref,
                 *, alpha):
    y = _norm_ple(alpha * x_ref[...] + ff_ref[...], p_ref[...], g_ref, b_ref, wg_ref, bg_ref,
                  wp_ref)
    out_ref[...] = y
    outb_ref[...] = y.astype(_BF)


def _post(x, ff, g, b, p, w_gate, b_gate, w_proj, alpha):
    n, d = x.shape
    pd = p.shape[1]
    tm = _pick(n, 512)
    tile = lambda width: pl.BlockSpec((tm, width), lambda i: (i, 0))
    return pl.pallas_call(
        functools.partial(_post_kernel, alpha=alpha),
        grid=(n // tm,),
        in_specs=[tile(d), tile(d), _resident((1, d)), _resident((1, d)), tile(pd),
                  _resident((d, d)), _resident((1, d)), _resident((pd, d))],
        out_specs=[tile(d), tile(d)],
        out_shape=[jax.ShapeDtypeStruct((n, d), _F32), jax.ShapeDtypeStruct((n, d), _BF)],
        compiler_params=_cparams("parallel"),
        name="post",
    )(x, ff, g.reshape(1, d), b.reshape(1, d), p, w_gate, b_gate.reshape(1, d), w_proj)


def _pw1_glu_kernel(x_ref, wv_ref, wg_ref, bv_ref, bg_ref, v_ref):
    for rows in _row_blocks(x_ref.shape[0]):
        xb = x_ref[rows, :]
        val = jnp.dot(xb, wv_ref[...], preferred_element_type=_F32) + bv_ref[...]
        gate = jnp.dot(xb, wg_ref[...], preferred_element_type=_F32) + bg_ref[...]
        v_ref[rows, :] = val * _sigmoid(gate)


def _pw1_glu(xb, w, bias):
    n, d = xb.shape
    dv = w.shape[1] // 2
    tm = _pick(n, 1024)
    tn = _pick(dv, 512, _LANES)
    nj = dv // tn
    bias = bias.reshape(1, 2 * dv)
    return pl.pallas_call(
        _pw1_glu_kernel,
        grid=(n // tm, nj),
        in_specs=[
            pl.BlockSpec((tm, d), lambda i, j: (i, 0)),
            pl.BlockSpec((d, tn), lambda i, j: (0, j)),
            pl.BlockSpec((d, tn), lambda i, j: (0, j + nj)),
            pl.BlockSpec((1, tn), lambda i, j: (0, j)),
            pl.BlockSpec((1, tn), lambda i, j: (0, j + nj)),
        ],
        out_specs=pl.BlockSpec((tm, tn), lambda i, j: (i, j)),
        out_shape=jax.ShapeDtypeStruct((n, dv), _F32),
        compiler_params=_cparams("parallel", "arbitrary"),
        name="pw1_glu",
    )(xb, w, w, bias, bias)


def _convmod_kernel(v_ref, buf0_ref, w_ref, b_ref, g_ref, beta_ref, y_ref, nbuf_ref, vp_scr,
                    c_scr, *, tc, ksz, hist):
    t = pl.program_id(1)
    d = v_ref.shape[-1]
    base = hist - (ksz - 1)

    @pl.when(t == 0)
    def _():
        vp_scr[0:hist, :] = buf0_ref[0]

    vp_scr[hist:hist + tc, :] = v_ref[0]
    rows = hist + tc
    for c in range(d // _LANES):
        sl = slice(c * _LANES, (c + 1) * _LANES)
        acc = jnp.broadcast_to(b_ref[:, sl], (tc, _LANES))
        whole = vp_scr[:, sl]
        for j in range(_SUBLANES):
            taps = [k for k in range(ksz) if (base + k) % _SUBLANES == j]
            if not taps:
                continue
            win = whole if j == 0 else pltpu.roll(whole, rows - j, 0)
            for k in taps:
                q0 = ((base + k) // _SUBLANES) * _SUBLANES
                acc = acc + w_ref[k:k + 1, sl] * win[q0:q0 + tc]
        c_scr[:, sl] = acc
    z = _layer_norm(c_scr[...], g_ref[...], beta_ref[...])
    y_ref[0] = (z * _sigmoid(z)).astype(y_ref.dtype)
    tail = vp_scr[tc:tc + hist, :]
    nbuf_ref[0] = tail
    vp_scr[0:hist, :] = tail


def _convmod(v, buf0, dw_w, dw_b, ln_g, ln_b):
    bsz, t, d = v.shape
    ksz = dw_w.shape[0]
    hist = -(-(ksz - 1) // _SUBLANES) * _SUBLANES
    assert t % _SUBLANES == 0 and d % _LANES == 0
    tc = _pick(t, 128)
    buf_pad = jnp.pad(buf0, ((0, 0), (hist - (ksz - 1), 0), (0, 0)))
    row = lambda a: a.reshape(1, d)
    seq = pl.BlockSpec((1, tc, d), lambda b, i: (b, i, 0))
    per_b = pl.BlockSpec((1, hist, d), lambda b, i: (b, 0, 0))
    y, nbuf = pl.pallas_call(
        functools.partial(_convmod_kernel, tc=tc, ksz=ksz, hist=hist),
        grid=(bsz, t // tc),
        in_specs=[seq, per_b, _resident((ksz, d)), _resident((1, d)), _resident((1, d)),
                  _resident((1, d))],
        out_specs=[seq, per_b],
        out_shape=[jax.ShapeDtypeStruct((bsz, t, d), _BF),
                   jax.ShapeDtypeStruct((bsz, hist, d), _F32)],
        scratch_shapes=[pltpu.VMEM((hist + tc, d), _F32), pltpu.VMEM((tc, d), _F32)],
        compiler_params=_cparams("parallel", "arbitrary"),
        name="convmod",
    )(v, buf_pad, dw_w, row(dw_b), row(ln_g), row(ln_b))
    return y, nbuf[:, hist - (ksz - 1):]


_META_EXPERT, _META_GATE, _META_RANK = 0, TOP_K, 2 * TOP_K


def _router_kernel(xb_ref, w_ref, gates_ref, meta_ref, counts_ref, run_scr, *, n_experts):
    @pl.when(pl.program_id(0) == 0)
    def _():
        run_scr[...] = jnp.zeros_like(run_scr)

    logits = jnp.dot(xb_ref[...], w_ref[...], preferred_element_type=_F32)
    tm = logits.shape[0]
    lane = lax.broadcasted_iota(jnp.int32, logits.shape, 1)
    valid = lane < n_experts
    logits = jnp.where(valid, logits, -1e30)
    ex = jnp.where(valid, jnp.exp(logits - jnp.max(logits, axis=1, keepdims=True)), 0.0)
    probs = ex / jnp.sum(ex, axis=1, keepdims=True)
    rest = jnp.where(valid, probs, -1.0)
    total = jnp.zeros_like(probs[:, :1])
    picks = []
    for _ in range(TOP_K):
        top = jnp.max(rest, axis=1, keepdims=True)
        idx = jnp.min(jnp.where(rest == top, lane, _LANES), axis=1, keepdims=True)
        hit = lane == idx
        picks.append((top, idx, hit))
        total = total + top
        rest = jnp.where(hit, -1.0, rest)
    gates = jnp.zeros_like(probs)
    chosen = jnp.zeros_like(probs)
    for top, _, hit in picks:
        gates = jnp.where(hit, top / total, gates)
        chosen = jnp.where(hit, 1.0, chosen)
    gates_ref[...] = gates
    r_i = lax.broadcasted_iota(jnp.int32, (tm, tm), 0)
    c_i = lax.broadcasted_iota(jnp.int32, (tm, tm), 1)
    tri = jnp.where(r_i > c_i, 1.0, 0.0).astype(_BF)
    before = jnp.dot(tri, chosen.astype(_BF), preferred_element_type=_F32) + run_scr[...]
    meta = jnp.zeros_like(probs)
    for k, (top, idx, hit) in enumerate(picks):
        rank = jnp.sum(jnp.where(hit, before, 0.0), axis=1, keepdims=True)
        meta = jnp.where(lane == _META_EXPERT + k, idx.astype(_F32), meta)
        meta = jnp.where(lane == _META_GATE + k, top / total, meta)
        meta = jnp.where(lane == _META_RANK + k, rank, meta)
    meta_ref[...] = meta
    run_scr[...] += jnp.sum(chosen, axis=0, keepdims=True)
    counts_ref[...] = run_scr[...]


def _router(xb, w_router):
    n, d = xb.shape
    ne = w_router.shape[1]
    assert ne <= _LANES and 3 * TOP_K <= _LANES
    wpad = jnp.pad(w_router, ((0, 0), (0, _LANES - ne))).astype(_BF)
    tm = _pick(n, 512)
    tile = pl.BlockSpec((tm, _LANES), lambda i: (i, 0))
    return pl.pallas_call(
        functools.partial(_router_kernel, n_experts=ne),
        grid=(n // tm,),
        in_specs=[pl.BlockSpec((tm, d), lambda i: (i, 0)), _resident((d, _LANES))],
        out_specs=[tile, tile, pl.BlockSpec((1, _LANES), lambda i: (0, 0))],
        out_shape=[jax.ShapeDtypeStruct((n, _LANES), _F32), jax.ShapeDtypeStruct((n, _LANES), _F32),
                   jax.ShapeDtypeStruct((1, _LANES), _F32)],
        scratch_shapes=[pltpu.VMEM((1, _LANES), _F32)],
        compiler_params=_cparams("arbitrary"),
        name="router",
    )(xb, wpad)


def _route_plan(meta, counts_row, n_experts, tg):
    n = meta.shape[0]
    expert = meta[:, _META_EXPERT:_META_EXPERT + TOP_K].astype(jnp.int32)
    rank = meta[:, _META_RANK:_META_RANK + TOP_K].astype(jnp.int32)
    counts = counts_row[0, :n_experts].astype(jnp.int32)
    padded = (counts + tg - 1) // tg * tg
    ends = jnp.cumsum(padded)
    starts = ends - padded
    pos = (jnp.take(starts, expert) + rank).reshape(-1)
    n_tiles = -(-(TOP_K * n + n_experts * (tg - 1)) // tg)
    tile_start = jnp.arange(n_tiles, dtype=jnp.int32) * tg
    tile_expert = jnp.sum((tile_start[:, None] >= ends[None, :]).astype(jnp.int32), axis=1)
    tile_expert = jnp.minimum(tile_expert, n_experts - 1)
    n_used = (ends[-1:] // tg).astype(jnp.int32)
    return pos, ends, tile_expert, n_used, n_tiles


def _dispatch_kernel(pos_ref, ends_ref, x_ref, out_hbm, sem, *, tm, tg, n_experts):
    base = pl.program_id(0) * (TOP_K * tm)

    def fill_tile(row0):
        for c in range(tg // tm):
            rows = pl.ds(pl.multiple_of(row0 + c * tm, tm), tm)
            fill = pltpu.make_async_copy(x_ref, out_hbm.at[rows], sem)
            fill.start()
            fill.wait()

    @pl.when(pl.program_id(0) == 0)
    def _():
        for e in range(n_experts):
            prev_end = ends_ref[e - 1] if e else 0

            @pl.when(ends_ref[e] > prev_end)
            def _():
                fill_tile(ends_ref[e] - tg)

        for u in range(1, min(n_experts, out_hbm.shape[0] // tg) + 1):
            row0 = out_hbm.shape[0] - u * tg

            @pl.when(row0 >= ends_ref[n_experts - 1])
            def _():
                fill_tile(row0)

    def issue(r, k):
        dst = pos_ref[base + TOP_K * r + k]
        pltpu.make_async_copy(x_ref.at[pl.ds(r, 1)], out_hbm.at[pl.ds(dst, 1)], sem).start()

    _for_each_row(tm, issue)
    for k in range(TOP_K):
        pltpu.make_async_copy(x_ref, out_hbm.at[pl.ds(0, tm)], sem).wait()


def _dispatch(x, pos, ends, n_rows, tg):
    n, d = x.shape
    tm = _pick(n, 256)
    assert tg % tm == 0
    return pl.pallas_call(
        functools.partial(_dispatch_kernel, tm=tm, tg=tg, n_experts=ends.shape[0]),
        grid_spec=pltpu.PrefetchScalarGridSpec(
            num_scalar_prefetch=2,
            grid=(n // tm,),
            in_specs=[pl.BlockSpec((tm, d), lambda i, pos, ends: (i, 0))],
            out_specs=pl.BlockSpec(memory_space=pl.ANY),
            scratch_shapes=[pltpu.SemaphoreType.DMA(())],
        ),
        out_shape=jax.ShapeDtypeStruct((n_rows, d), x.dtype),
        compiler_params=_cparams("arbitrary"),
        name="dispatch",
    )(pos, ends, x)


def _grouped_ffn_kernel(te_ref, nu_ref, x_ref, wi_ref, wo_ref, out_ref, *, chunk):
    del te_ref
    f = wo_ref.shape[1]

    @pl.when(pl.program_id(0) < nu_ref[0])
    def _():
        xb = x_ref[...].astype(_BF)
        for c0 in range(0, f, chunk):
            c1 = min(c0 + chunk, f)
            g = jnp.dot(xb, wi_ref[0, :, c0:c1], preferred_element_type=_F32)
            u = jnp.dot(xb, wi_ref[0, :, f + c0:f + c1], preferred_element_type=_F32)
            h = ((g * _sigmoid(g)) * u).astype(_BF)
            part = jnp.dot(h, wo_ref[0, c0:c1, :], preferred_element_type=_F32)
            if c0 == 0:
                out_ref[...] = part
            else:
                out_ref[...] += part

    @pl.when(pl.program_id(0) >= nu_ref[0])
    def _():
        out_ref[...] = jnp.zeros_like(out_ref)


def _grouped_ffn(xs, w_in, w_out, tile_expert, n_used, tg):
    r, d = xs.shape
    f = w_out.shape[1]
    expert_block = lambda shape: pl.BlockSpec(shape, lambda i, te, nu: (te[i], 0, 0),
                                              pipeline_mode=pl.Buffered(1))
    return pl.pallas_call(
        functools.partial(_grouped_ffn_kernel, chunk=2 * _V7X_MXU_DIM),
        grid_spec=pltpu.PrefetchScalarGridSpec(
            num_scalar_prefetch=2,
            grid=(r // tg,),
            in_specs=[
                pl.BlockSpec((tg, d), lambda i, te, nu: (jnp.minimum(i, nu[0] - 1), 0)),
                expert_block((1, d, 2 * f)),
                expert_block((1, f, d)),
            ],
            out_specs=pl.BlockSpec((tg, d), lambda i, te, nu: (i, 0)),
        ),
        out_shape=jax.ShapeDtypeStruct((r, d), _F32),
        compiler_params=pltpu.CompilerParams(dimension_semantics=("arbitrary",),
                                             vmem_limit_bytes=_VMEM_LIMIT_LARGE),
        name="grouped_ffn",
    )(tile_expert, n_used, xs, w_in, w_out)


def _moe_post_kernel(pos_ref, x_ref, meta_ref, ys_hbm, g_ref, b_ref, p_ref, wg_ref, bg_ref, wp_ref,
                     out_ref, outb_ref, buf, sem, *, alpha, tm, n_tiles):
    i = pl.program_id(0)
    slot = i % 2

    def gather(tile, to_slot):
        base = tile * (TOP_K * tm)

        def issue(r, k):
            src = pos_ref[base + TOP_K * r + k]
            pltpu.make_async_copy(ys_hbm.at[pl.ds(src, 1)], buf.at[to_slot, k, pl.ds(r, 1)],
                                  sem.at[to_slot]).start()

        _for_each_row(tm, issue)

    @pl.when(i == 0)
    def _():
        gather(0, 0)

    @pl.when(i + 1 < n_tiles)
    def _():
        gather(i + 1, 1 - slot)

    for k in range(TOP_K):
        pltpu.make_async_copy(ys_hbm.at[pl.ds(0, tm)], buf.at[slot, k], sem.at[slot]).wait()
    meta = meta_ref[...]
    ff = meta[:, _META_GATE:_META_GATE + 1] * buf[slot, 0]
    for k in range(1, TOP_K):
        ff = ff + meta[:, _META_GATE + k:_META_GATE + k + 1] * buf[slot, k]
    y = _norm_ple(alpha * x_ref[...] + ff, p_ref[...], g_ref, b_ref, wg_ref, bg_ref, wp_ref)
    out_ref[...] = y
    outb_ref[...] = y.astype(_BF)


def _moe_post(x, ys, pos, meta, g, b, p, w_gate, b_gate, w_proj, alpha):
    n, d = x.shape
    pd = p.shape[1]
    tm = _pick(n, 256)
    tile = lambda width: pl.BlockSpec((tm, width), lambda i, pos: (i, 0))
    res = lambda shape: pl.BlockSpec(shape, lambda i, pos: (0,) * len(shape),
                                     pipeline_mode=pl.Buffered(1))
    return pl.pallas_call(
        functools.partial(_moe_post_kernel, alpha=alpha, tm=tm, n_tiles=n // tm),
        grid_spec=pltpu.PrefetchScalarGridSpec(
            num_scalar_prefetch=1,
            grid=(n // tm,),
            in_specs=[tile(d), tile(_LANES), pl.BlockSpec(memory_space=pl.ANY), res((1, d)),
                      res((1, d)), tile(pd), res((d, d)), res((1, d)), res((pd, d))],
            out_specs=[tile(d), tile(d)],
            scratch_shapes=[pltpu.VMEM((2, TOP_K, tm, d), _F32), pltpu.SemaphoreType.DMA((2,))],
        ),
        out_shape=[jax.ShapeDtypeStruct((n, d), _F32), jax.ShapeDtypeStruct((n, d), _BF)],
        compiler_params=_cparams("arbitrary"),
        name="moe_post",
    )(pos, x, meta, ys, g.reshape(1, d), b.reshape(1, d), p, w_gate, b_gate.reshape(1, d), w_proj)


_SPARSE_MOE_MIN_TOKENS = 4096
_GROUP_TILE_ROWS = 512


def _trunk(x, p, lru_conv, lru_h, cm_conv, w):
    bsz, t, d = x.shape
    depth = p.shape[0]
    n = bsz * t
    alpha = (2.0 * depth) ** 0.25
    n_mixers = 2
    x = x.reshape(n, d)
    xb = x
    new_lru_conv, new_lru_h, new_cm_conv = [], [], []
    for i in range(depth):
        j = i // n_mixers
        if i % n_mixers == 0:
            lw = w['rglru_w_out'].shape[1]
            gate, rec = _in_proj(xb, w['rglru_w_in'][j])
            y, buf, hl = _rglru(rec.reshape(bsz, t, lw), gate.reshape(bsz, t, lw), lru_conv[j],
                                lru_h[j], w['rglru_conv_w'][j], w['rglru_conv_b'][j],
                                w['rglru_w_a'][j], w['rglru_b_a'][j], w['rglru_w_x'][j],
                                w['rglru_b_x'][j], w['rglru_lambda'][j])
            new_lru_conv.append(buf)
            new_lru_h.append(hl)
            x, xb = _proj_ln(y.reshape(n, lw), w['rglru_w_out'][j], jnp.zeros((d,), _F32), x,
                             w['ln_mix_g'][i], w['ln_mix_b'][i], alpha)
        else:
            v = _pw1_glu(xb.astype(_BF), w['cm_w_pw1'][j], w['cm_b_pw1'][j])
            y, buf = _convmod(v.reshape(bsz, t, d), cm_conv[j], w['cm_dw_w'][j], w['cm_dw_b'][j],
                              w['cm_ln_g'][j], w['cm_ln_b'][j])
            new_cm_conv.append(buf)
            x, xb = _proj_ln(y.reshape(n, d), w['cm_w_pw2'][j], w['cm_b_pw2'][j], x,
                             w['ln_mix_g'][i], w['ln_mix_b'][i], alpha)
        k = i // 2
        post_args = (w['ln_ffn_g'][i], w['ln_ffn_b'][i], p[i].reshape(n, -1), w['ple_w_gate'][i],
                     w['ple_b_gate'][i], w['ple_w_proj'][i], alpha)
        if i % 2 == 0:
            ff = _ffn(xb, w['ffn_w_in'][k][None], w['ffn_w_out'][k][None])
            x, xb = _post(x, ff, *post_args)
        else:
            gates, meta, counts = _router(xb, w['moe_w_router'][k])
            ne = w['moe_w_router'].shape[-1]
            if n < _SPARSE_MOE_MIN_TOKENS:
                ff = _ffn(xb, w['moe_w_in'][k], w['moe_w_out'][k], gates, tf_pref=1408)
                x, xb = _post(x, ff, *post_args)
            else:
                tg = _GROUP_TILE_ROWS
                pos, ends, tile_expert, n_used, n_tiles = _route_plan(meta, counts, ne, tg)
                xs = _dispatch(x, pos, ends, n_tiles * tg, tg)
                ys = _grouped_ffn(xs, w['moe_w_in'][k], w['moe_w_out'][k], tile_expert, n_used,
                                  tg)
                x, xb = _moe_post(x, ys, pos, meta, *post_args)
    return (x.reshape(bsz, t, d), jnp.stack(new_lru_conv), jnp.stack(new_lru_h),
            jnp.stack(new_cm_conv))


_MATMUL_WEIGHTS = ('rglru_w_in', 'rglru_w_a', 'rglru_w_x', 'rglru_w_out', 'cm_w_pw1', 'cm_w_pw2',
                   'ffn_w_in', 'ffn_w_out', 'moe_w_in', 'moe_w_out', 'ple_w_proj', 'ple_w_gate')


def kernel(x_prompt, x_sample, p_prompt, p_sample, state_rglru_conv, state_rglru_h, state_conv_module, rglru_w_in, rglru_conv_w, rglru_conv_b, rglru_w_a, rglru_b_a, rglru_w_x, rglru_b_x, rglru_lambda, rglru_w_out, cm_w_pw1, cm_b_pw1, cm_dw_w, cm_dw_b, cm_ln_g, cm_ln_b, cm_w_pw2, cm_b_pw2, ffn_w_in, ffn_w_out, moe_w_router, moe_w_in, moe_w_out, ln_mix_g, ln_mix_b, ln_ffn_g, ln_ffn_b, ple_w_proj, ple_w_gate, ple_b_gate):
    w = dict(rglru_w_in=rglru_w_in, rglru_conv_w=rglru_conv_w, rglru_conv_b=rglru_conv_b,
             rglru_w_a=rglru_w_a, rglru_b_a=rglru_b_a, rglru_w_x=rglru_w_x, rglru_b_x=rglru_b_x,
             rglru_lambda=rglru_lambda, rglru_w_out=rglru_w_out, cm_w_pw1=cm_w_pw1,
             cm_b_pw1=cm_b_pw1, cm_dw_w=cm_dw_w, cm_dw_b=cm_dw_b, cm_ln_g=cm_ln_g,
             cm_ln_b=cm_ln_b, cm_w_pw2=cm_w_pw2, cm_b_pw2=cm_b_pw2, ffn_w_in=ffn_w_in,
             ffn_w_out=ffn_w_out, moe_w_router=moe_w_router, moe_w_in=moe_w_in,
             moe_w_out=moe_w_out, ln_mix_g=ln_mix_g, ln_mix_b=ln_mix_b, ln_ffn_g=ln_ffn_g,
             ln_ffn_b=ln_ffn_b, ple_w_proj=ple_w_proj, ple_w_gate=ple_w_gate,
             ple_b_gate=ple_b_gate)
    for name in _MATMUL_WEIGHTS:
        w[name] = w[name].astype(_BF)
    bp = x_prompt.shape[0]
    dt = x_prompt.dtype
    zero_conv = jnp.zeros((state_rglru_conv.shape[0], bp) + state_rglru_conv.shape[2:], dt)
    zero_h = jnp.zeros((state_rglru_h.shape[0], bp) + state_rglru_h.shape[2:], dt)
    zero_cm = jnp.zeros((state_conv_module.shape[0], bp) + state_conv_module.shape[2:], dt)
    y_p, conv_p, h_p, cm_p = _trunk(x_prompt, p_prompt, zero_conv, zero_h, zero_cm, w)
    y_s, conv_s, h_s, cm_s = _trunk(x_sample, p_sample, state_rglru_conv, state_rglru_h,
                                    state_conv_module, w)
    return (y_p, y_s, conv_p, h_p, cm_p, conv_s, h_s, cm_s)
```

```python
import functools

import jax
import jax.numpy as jnp
from jax import lax
from jax.experimental import pallas as pl
from jax.experimental.pallas import tpu as pltpu

_BF = jnp.bfloat16
_F32 = jnp.float32

LRU_C = 8.0
LN_EPS = 1e-5
TOP_K = 2

_V7X_VMEM_BYTES = 64 * 1024 * 1024
_VMEM_LIMIT = _V7X_VMEM_BYTES - 8 * 1024 * 1024
_VMEM_LIMIT_LARGE = _V7X_VMEM_BYTES - 4 * 1024 * 1024
_SUBLANES = 8
_LANES = 128
_V7X_MXU_DIM = 256
_ROW_BLOCK = 128


def _cparams(*sem):
    return pltpu.CompilerParams(dimension_semantics=sem, vmem_limit_bytes=_VMEM_LIMIT)


def _pick(n, pref, mult=_SUBLANES):
    best = None
    for c in range(mult, min(n, pref) + 1, mult):
        if n % c == 0:
            best = c
    assert best is not None, (n, pref, mult)
    return best


def _row_blocks(tm):
    rb = _pick(tm, _ROW_BLOCK)
    return [slice(r0, r0 + rb) for r0 in range(0, tm, rb)]


def _for_each_row(tm, issue):
    def group(g, c):
        r0 = pl.multiple_of(g * _SUBLANES, _SUBLANES)
        for j in range(_SUBLANES):
            for k in range(TOP_K):
                issue(r0 + j, k)
        return c

    lax.fori_loop(0, tm // _SUBLANES, group, 0)


def _resident(shape):
    nd = len(shape)
    return pl.BlockSpec(shape, lambda *_: (0,) * nd, pipeline_mode=pl.Buffered(1))


def _layer_norm(x, g, b):
    mu = jnp.mean(x, axis=-1, keepdims=True)
    xc = x - mu
    var = jnp.mean(xc * xc, axis=-1, keepdims=True)
    return xc * lax.rsqrt(var + LN_EPS) * g + b


def _sigmoid(x):
    return 1.0 / (1.0 + jnp.exp(-x))


def _gelu_tanh(x):
    c = 0.7978845608028654
    return 0.5 * x * (1.0 + jnp.tanh(c * (x + 0.044715 * (x * x * x))))


def _softplus(x):
    return jnp.maximum(x, 0.0) + jnp.log1p(jnp.exp(-jnp.abs(x)))


def _in_proj_kernel(x_ref, wg_ref, wr_ref, gate_ref, rec_ref, xb_scr):
    @pl.when(pl.program_id(1) == 0)
    def _():
        xb_scr[...] = x_ref[...].astype(_BF)

    for rows in _row_blocks(x_ref.shape[0]):
        xb = xb_scr[rows, :]
        gate_ref[rows, :] = _gelu_tanh(jnp.dot(xb, wg_ref[...], preferred_element_type=_F32))
        rec_ref[rows, :] = jnp.dot(xb, wr_ref[...], preferred_element_type=_F32)


def _in_proj(x, w):
    n, d = x.shape
    lw = w.shape[1] // 2
    tm = _pick(n, 1024)
    tn = _pick(lw, 512, _LANES)
    nj = lw // tn
    return pl.pallas_call(
        _in_proj_kernel,
        grid=(n // tm, nj),
        in_specs=[
            pl.BlockSpec((tm, d), lambda i, j: (i, 0)),
            pl.BlockSpec((d, tn), lambda i, j: (0, j)),
            pl.BlockSpec((d, tn), lambda i, j: (0, j + nj)),
        ],
        out_specs=[
            pl.BlockSpec((tm, tn), lambda i, j: (i, j)),
            pl.BlockSpec((tm, tn), lambda i, j: (i, j)),
        ],
        out_shape=[jax.ShapeDtypeStruct((n, lw), _F32)] * 2,
        scratch_shapes=[pltpu.VMEM((tm, d), _BF)],
        compiler_params=_cparams("parallel", "arbitrary"),
        name="in_proj",
    )(x, w, w)


def _rglru_kernel(rec_ref, gate_ref, buf0_ref, h0_ref, cw_ref, cb_ref, wa_ref, ba_ref, wx_ref,
                  bx_ref, lam_ref, y_ref, nbuf_ref, hlast_ref, xp_scr, a_scr, b_scr, carry_scr,
                  *, tc, heads, kw):
    t = pl.program_id(1)
    lw = rec_ref.shape[-1]
    blk = lw // heads
    hist = _SUBLANES
    base = hist - (kw - 1)

    @pl.when(t == 0)
    def _():
        xp_scr[0:hist, :] = buf0_ref[0]
        carry_scr[...] = jnp.broadcast_to(h0_ref[0], (_SUBLANES, lw))

    xp_scr[hist:hist + tc, :] = rec_ref[0]
    decay = -LRU_C * _softplus(-lam_ref[...])
    for h in range(heads):
        sl = slice(h * blk, (h + 1) * blk)
        whole = xp_scr[:, sl]
        xc = jnp.broadcast_to(cb_ref[:, sl], (tc, blk))
        for k in range(kw):
            j = (base + k) % _SUBLANES
            q0 = (base + k) - j
            win = whole if j == 0 else pltpu.roll(whole, hist + tc - j, 0)
            xc = xc + cw_ref[k:k + 1, sl] * win[q0:q0 + tc]
        xcb = xc.astype(_BF)
        r = _sigmoid(jnp.dot(xcb, wa_ref[h], preferred_element_type=_F32) + ba_ref[:, sl])
        ig = _sigmoid(jnp.dot(xcb, wx_ref[h], preferred_element_type=_F32) + bx_ref[:, sl])
        log_a = r * decay[:, sl]
        a = jnp.exp(log_a)
        a_scr[:, sl] = a
        b_scr[:, sl] = jnp.sqrt(-jnp.tanh(log_a) * (a * a + 1.0)) * (ig * xc)

    row = lax.broadcasted_iota(jnp.int32, (_SUBLANES, lw), 0)

    def body(g, carry):
        r0 = pl.multiple_of(g * _SUBLANES, _SUBLANES)
        a = a_scr[pl.ds(r0, _SUBLANES), :]
        b = b_scr[pl.ds(r0, _SUBLANES), :]
        for s in (1, 2, 4):
            m = row >= s
            a_sh = jnp.where(m, pltpu.roll(a, s, 0), 1.0)
            b_sh = jnp.where(m, pltpu.roll(b, s, 0), 0.0)
            b = a * b_sh + b
            a = a * a_sh
        hrows = a * carry + b
        b_scr[pl.ds(r0, _SUBLANES), :] = hrows
        return jnp.broadcast_to(hrows[_SUBLANES - 1:_SUBLANES, :], (_SUBLANES, lw))

    carry = lax.fori_loop(0, tc // _SUBLANES, body, carry_scr[...])
    carry_scr[...] = carry
    y_ref[0] = (b_scr[...] * gate_ref[0]).astype(y_ref.dtype)
    hlast_ref[0] = carry
    tail = xp_scr[tc:tc + hist, :]
    nbuf_ref[0] = tail
    xp_scr[0:hist, :] = tail


def _rglru(rec, gate, buf0, h0, conv_w, conv_b, w_a, b_a, w_x, b_x, lam):
    bsz, t, lw = rec.shape
    heads, blk, _ = w_a.shape
    kw = conv_w.shape[0]
    assert kw - 1 <= _SUBLANES <= t and t % _SUBLANES == 0
    tc = _pick(t, 256)
    buf_pad = jnp.pad(buf0, ((0, 0), (_SUBLANES - (kw - 1), 0), (0, 0)))
    row = lambda v: v.reshape(1, lw)
    seq = pl.BlockSpec((1, tc, lw), lambda b, i: (b, i, 0))
    per_b = pl.BlockSpec((1, _SUBLANES, lw), lambda b, i: (b, 0, 0))
    y, nbuf, hlast = pl.pallas_call(
        functools.partial(_rglru_kernel, tc=tc, heads=heads, kw=kw),
        grid=(bsz, t // tc),
        in_specs=[
            seq, seq, per_b,
            pl.BlockSpec((1, 1, lw), lambda b, i: (b, 0, 0)),
            _resident((kw, lw)), _resident((1, lw)),
            _resident((heads, blk, blk)), _resident((1, lw)),
            _resident((heads, blk, blk)), _resident((1, lw)),
            _resident((1, lw)),
        ],
        out_specs=[seq, per_b, per_b],
        out_shape=[
            jax.ShapeDtypeStruct((bsz, t, lw), _BF),
            jax.ShapeDtypeStruct((bsz, _SUBLANES, lw), _F32),
            jax.ShapeDtypeStruct((bsz, _SUBLANES, lw), _F32),
        ],
        scratch_shapes=[
            pltpu.VMEM((_SUBLANES + tc, lw), _F32),
            pltpu.VMEM((tc, lw), _F32),
            pltpu.VMEM((tc, lw), _F32),
            pltpu.VMEM((_SUBLANES, lw), _F32),
        ],
        compiler_params=_cparams("parallel", "arbitrary"),
        name="rglru",
    )(rec, gate, buf_pad, h0.reshape(bsz, 1, lw), conv_w, row(conv_b), w_a, row(b_a), w_x,
      row(b_x), row(lam))
    return y, nbuf[:, _SUBLANES - (kw - 1):], hlast[:, 0]


def _proj_ln_kernel(a_ref, w_ref, bias_ref, x_ref, g_ref, b_ref, out_ref, outb_ref, *, alpha):
    for rows in _row_blocks(a_ref.shape[0]):
        mix = jnp.dot(a_ref[rows, :], w_ref[...], preferred_element_type=_F32) + bias_ref[...]
        y = _layer_norm(alpha * x_ref[rows, :] + mix, g_ref[...], b_ref[...])
        out_ref[rows, :] = y
        outb_ref[rows, :] = y.astype(_BF)


def _proj_ln(a, w, bias, x, g, b, alpha):
    n, k = a.shape
    d = w.shape[1]
    tm = _pick(n, 512)
    tile = lambda width: pl.BlockSpec((tm, width), lambda i: (i, 0))
    return pl.pallas_call(
        functools.partial(_proj_ln_kernel, alpha=alpha),
        grid=(n // tm,),
        in_specs=[tile(k), _resident((k, d)), _resident((1, d)), tile(d), _resident((1, d)),
                  _resident((1, d))],
        out_specs=[tile(d), tile(d)],
        out_shape=[jax.ShapeDtypeStruct((n, d), _F32), jax.ShapeDtypeStruct((n, d), _BF)],
        compiler_params=_cparams("parallel"),
        name="proj_ln",
    )(a, w, bias.reshape(1, d), x, g.reshape(1, d), b.reshape(1, d))


def _ffn_kernel(x_ref, wg_ref, wu_ref, wo_ref, *rest, gated):
    if gated:
        gates_ref, out_ref = rest
    else:
        (out_ref,) = rest
    e = pl.program_id(1)
    f = pl.program_id(2)

    @pl.when((e == 0) & (f == 0))
    def _():
        out_ref[...] = jnp.zeros_like(out_ref)

    xb = x_ref[...]
    g = jnp.dot(xb, wg_ref[0], preferred_element_type=_F32)
    u = jnp.dot(xb, wu_ref[0], preferred_element_type=_F32)
    h = (g * _sigmoid(g)) * u
    if gated:
        lane = lax.broadcasted_iota(jnp.int32, gates_ref.shape, 1)
        h = h * jnp.sum(jnp.where(lane == e, gates_ref[...], 0.0), axis=1, keepdims=True)
    out_ref[...] += jnp.dot(h.astype(_BF), wo_ref[0], preferred_element_type=_F32)


def _ffn(xb, w_in, w_out, gates=None, tf_pref=512):
    n, d = xb.shape
    ne, f, _ = w_out.shape
    tm = _pick(n, 1024)
    tf = _pick(f, tf_pref, _LANES)
    nf = f // tf
    in_specs = [
        pl.BlockSpec((tm, d), lambda i, e, j: (i, 0)),
        pl.BlockSpec((1, d, tf), lambda i, e, j: (e, 0, j)),
        pl.BlockSpec((1, d, tf), lambda i, e, j: (e, 0, j + nf)),
        pl.BlockSpec((1, tf, d), lambda i, e, j: (e, j, 0)),
    ]
    args = [xb, w_in, w_in, w_out]
    if gates is not None:
        in_specs.append(pl.BlockSpec((tm, _LANES), lambda i, e, j: (i, 0)))
        args.append(gates)
    return pl.pallas_call(
        functools.partial(_ffn_kernel, gated=gates is not None),
        grid=(n // tm, ne, nf),
        in_specs=in_specs,
        out_specs=pl.BlockSpec((tm, d), lambda i, e, j: (i, 0)),
        out_shape=jax.ShapeDtypeStruct((n, d), _F32),
        compiler_params=_cparams("parallel", "arbitrary", "arbitrary"),
        name="ffn",
    )(*args)


def _norm_ple(pre, p, g_ref, b_ref, wg_ref, bg_ref, wp_ref):
    xn = _layer_norm(pre, g_ref[...], b_ref[...])
    gate = _sigmoid(jnp.dot(xn.astype(_BF), wg_ref[...], preferred_element_type=_F32)
                    + bg_ref[...])
    proj = jnp.dot(p.astype(_BF), wp_ref[...], preferred_element_type=_F32)
    return xn + gate * proj


def _post_kernel(x_ref, ff_ref, g_ref, b_ref, p_ref, wg_ref, bg_ref, wp_ref, out_ref, outb_ref,
                 *, alpha):
    y = _norm_ple(alpha * x_ref[...] + ff_ref[...], p_ref[...], g_ref, b_ref, wg_ref, bg_ref,
                  wp_ref)
    out_ref[...] = y
    outb_ref[...] = y.astype(_BF)


def _post(x, ff, g, b, p, w_gate, b_gate, w_proj, alpha):
    n, d = x.shape
    pd = p.shape[1]
    tm = _pick(n, 512)
    tile = lambda width: pl.BlockSpec((tm, width), lambda i: (i, 0))
    return pl.pallas_call(
        functools.partial(_post_kernel, alpha=alpha),
        grid=(n // tm,),
        in_specs=[tile(d), tile(d), _resident((1, d)), _resident((1, d)), tile(pd),
                  _resident((d, d)), _resident((1, d)), _resident((pd, d))],
        out_specs=[tile(d), tile(d)],
        out_shape=[jax.ShapeDtypeStruct((n, d), _F32), jax.ShapeDtypeStruct((n, d), _BF)],
        compiler_params=_cparams("parallel"),
        name="post",
    )(x, ff, g.reshape(1, d), b.reshape(1, d), p, w_gate, b_gate.reshape(1, d), w_proj)


def _pw1_glu_kernel(x_ref, wv_ref, wg_ref, bv_ref, bg_ref, v_ref):
    for rows in _row_blocks(x_ref.shape[0]):
        xb = x_ref[rows, :]
        val = jnp.dot(xb, wv_ref[...], preferred_element_type=_F32) + bv_ref[...]
        gate = jnp.dot(xb, wg_ref[...], preferred_element_type=_F32) + bg_ref[...]
        v_ref[rows, :] = val * _sigmoid(gate)


def _pw1_glu(xb, w, bias):
    n, d = xb.shape
    dv = w.shape[1] // 2
    tm = _pick(n, 1024)
    tn = _pick(dv, 512, _LANES)
    nj = dv // tn
    bias = bias.reshape(1, 2 * dv)
    return pl.pallas_call(
        _pw1_glu_kernel,
        grid=(n // tm, nj),
        in_specs=[
            pl.BlockSpec((tm, d), lambda i, j: (i, 0)),
            pl.BlockSpec((d, tn), lambda i, j: (0, j)),
            pl.BlockSpec((d, tn), lambda i, j: (0, j + nj)),
            pl.BlockSpec((1, tn), lambda i, j: (0, j)),
            pl.BlockSpec((1, tn), lambda i, j: (0, j + nj)),
        ],
        out_specs=pl.BlockSpec((tm, tn), lambda i, j: (i, j)),
        out_shape=jax.ShapeDtypeStruct((n, dv), _F32),
        compiler_params=_cparams("parallel", "arbitrary"),
        name="pw1_glu",
    )(xb, w, w, bias, bias)


def _convmod_kernel(v_ref, buf0_ref, w_ref, b_ref, g_ref, beta_ref, y_ref, nbuf_ref, vp_scr,
                    c_scr, *, tc, ksz, hist):
    t = pl.program_id(1)
    d = v_ref.shape[-1]
    base = hist - (ksz - 1)

    @pl.when(t == 0)
    def _():
        vp_scr[0:hist, :] = buf0_ref[0]

    vp_scr[hist:hist + tc, :] = v_ref[0]
    rows = hist + tc
    for c in range(d // _LANES):
        sl = slice(c * _LANES, (c + 1) * _LANES)
        acc = jnp.broadcast_to(b_ref[:, sl], (tc, _LANES))
        whole = vp_scr[:, sl]
        for j in range(_SUBLANES):
            taps = [k for k in range(ksz) if (base + k) % _SUBLANES == j]
            if not taps:
                continue
            win = whole if j == 0 else pltpu.roll(whole, rows - j, 0)
            for k in taps:
                q0 = ((base + k) // _SUBLANES) * _SUBLANES
                acc = acc + w_ref[k:k + 1, sl] * win[q0:q0 + tc]
        c_scr[:, sl] = acc
    z = _layer_norm(c_scr[...], g_ref[...], beta_ref[...])
    y_ref[0] = (z * _sigmoid(z)).astype(y_ref.dtype)
    tail = vp_scr[tc:tc + hist, :]
    nbuf_ref[0] = tail
    vp_scr[0:hist, :] = tail


def _depthwise_conv(vp_scr, r0, nrows, hist, w_ref, b_ref, ksz, c_scr):
    d = vp_scr.shape[-1]
    base = hist - (ksz - 1)
    rows = hist + nrows
    for c in range(d // _LANES):
        sl = slice(c * _LANES, (c + 1) * _LANES)
        acc = jnp.broadcast_to(b_ref[:, sl], (nrows, _LANES))
        whole = vp_scr[r0:r0 + rows, sl]
        for j in range(_SUBLANES):
            taps = [k for k in range(ksz) if (base + k) % _SUBLANES == j]
            if not taps:
                continue
            win = whole if j == 0 else pltpu.roll(whole, rows - j, 0)
            for k in taps:
                q0 = ((base + k) // _SUBLANES) * _SUBLANES
                acc = acc + w_ref[k:k + 1, sl] * win[q0:q0 + nrows]
        c_scr[r0:r0 + nrows, sl] = acc


def _conv_layer_kernel(xb_ref, x_ref, buf0_ref, w1_ref, b1_ref, dw_ref, db_ref, cg_ref, cb_ref,
                       w2_ref, b2_ref, g_ref, b_ref, out_ref, outb_ref, nbuf_ref, vp_scr, c_scr,
                       *, tc, sub, ksz, hist, alpha):
    d = x_ref.shape[-1]

    @pl.when(pl.program_id(1) == 0)
    def _():
        vp_scr[0:hist, :] = buf0_ref[0]

    for r0 in range(0, tc, sub):
        rows = slice(r0, r0 + sub)
        xb = xb_ref[0, rows, :]
        val = jnp.dot(xb, w1_ref[:, :d], preferred_element_type=_F32) + b1_ref[:, :d]
        gate = jnp.dot(xb, w1_ref[:, d:], preferred_element_type=_F32) + b1_ref[:, d:]
        vp_scr[hist + r0:hist + r0 + sub, :] = val * _sigmoid(gate)
        _depthwise_conv(vp_scr, r0, sub, hist, dw_ref, db_ref, ksz, c_scr)
        z = _layer_norm(c_scr[rows, :], cg_ref[...], cb_ref[...])
        y = (z * _sigmoid(z)).astype(_BF)
        mix = jnp.dot(y, w2_ref[...], preferred_element_type=_F32) + b2_ref[...]
        xo = _layer_norm(alpha * x_ref[0, rows, :] + mix, g_ref[...], b_ref[...])
        out_ref[0, rows, :] = xo
        outb_ref[0, rows, :] = xo.astype(_BF)
    tail = vp_scr[tc:tc + hist, :]
    nbuf_ref[0] = tail
    vp_scr[0:hist, :] = tail


def _conv_layer(xb, x, buf0, w_pw1, b_pw1, dw_w, dw_b, cm_g, cm_b, w_pw2, b_pw2, ln_g, ln_b,
                alpha):
    bsz, t, d = x.shape
    ksz = dw_w.shape[0]
    hist = -(-(ksz - 1) // _SUBLANES) * _SUBLANES
    assert t % _SUBLANES == 0 and d % _LANES == 0
    tc = _pick(t, 256)
    sub = _pick(tc, 128)
    buf_pad = jnp.pad(buf0, ((0, 0), (hist - (ksz - 1), 0), (0, 0)))
    row = lambda a: a.reshape(1, -1)
    seq = pl.BlockSpec((1, tc, d), lambda b, i: (b, i, 0))
    per_b = pl.BlockSpec((1, hist, d), lambda b, i: (b, 0, 0))
    out, outb, nbuf = pl.pallas_call(
        functools.partial(_conv_layer_kernel, tc=tc, sub=sub, ksz=ksz, hist=hist, alpha=alpha),
        grid=(bsz, t // tc),
        in_specs=[seq, seq, per_b, _resident((d, 2 * d)), _resident((1, 2 * d)),
                  _resident((ksz, d)), _resident((1, d)), _resident((1, d)), _resident((1, d)),
                  _resident((d, d)), _resident((1, d)), _resident((1, d)), _resident((1, d))],
        out_specs=[seq, seq, per_b],
        out_shape=[jax.ShapeDtypeStruct((bsz, t, d), _F32), jax.ShapeDtypeStruct((bsz, t, d), _BF),
                   jax.ShapeDtypeStruct((bsz, hist, d), _F32)],
        scratch_shapes=[pltpu.VMEM((hist + tc, d), _F32), pltpu.VMEM((tc, d), _F32)],
        compiler_params=_cparams("parallel", "arbitrary"),
        name="conv_layer",
    )(xb, x, buf_pad, w_pw1, row(b_pw1), dw_w, row(dw_b), row(cm_g), row(cm_b), w_pw2,
      row(b_pw2), row(ln_g), row(ln_b))
    return out, outb, nbuf[:, hist - (ksz - 1):]


def _convmod(v, buf0, dw_w, dw_b, ln_g, ln_b):
    bsz, t, d = v.shape
    ksz = dw_w.shape[0]
    hist = -(-(ksz - 1) // _SUBLANES) * _SUBLANES
    assert t % _SUBLANES == 0 and d % _LANES == 0
    tc = _pick(t, 128)
    buf_pad = jnp.pad(buf0, ((0, 0), (hist - (ksz - 1), 0), (0, 0)))
    row = lambda a: a.reshape(1, d)
    seq = pl.BlockSpec((1, tc, d), lambda b, i: (b, i, 0))
    per_b = pl.BlockSpec((1, hist, d), lambda b, i: (b, 0, 0))
    y, nbuf = pl.pallas_call(
        functools.partial(_convmod_kernel, tc=tc, ksz=ksz, hist=hist),
        grid=(bsz, t // tc),
        in_specs=[seq, per_b, _resident((ksz, d)), _resident((1, d)), _resident((1, d)),
                  _resident((1, d))],
        out_specs=[seq, per_b],
        out_shape=[jax.ShapeDtypeStruct((bsz, t, d), _BF),
                   jax.ShapeDtypeStruct((bsz, hist, d), _F32)],
        scratch_shapes=[pltpu.VMEM((hist + tc, d), _F32), pltpu.VMEM((tc, d), _F32)],
        compiler_params=_cparams("parallel", "arbitrary"),
        name="convmod",
    )(v, buf_pad, dw_w, row(dw_b), row(ln_g), row(ln_b))
    return y, nbuf[:, hist - (ksz - 1):]


_META_EXPERT, _META_GATE, _META_RANK = 0, TOP_K, 2 * TOP_K


def _router_kernel(xb_ref, w_ref, gates_ref, meta_ref, counts_ref, run_scr, *, n_experts):
    @pl.when(pl.program_id(0) == 0)
    def _():
        run_scr[...] = jnp.zeros_like(run_scr)

    logits = jnp.dot(xb_ref[...], w_ref[...], preferred_element_type=_F32)
    tm = logits.shape[0]
    lane = lax.broadcasted_iota(jnp.int32, logits.shape, 1)
    valid = lane < n_experts
    logits = jnp.where(valid, logits, -1e30)
    ex = jnp.where(valid, jnp.exp(logits - jnp.max(logits, axis=1, keepdims=True)), 0.0)
    probs = ex / jnp.sum(ex, axis=1, keepdims=True)
    rest = jnp.where(valid, probs, -1.0)
    total = jnp.zeros_like(probs[:, :1])
    picks = []
    for _ in range(TOP_K):
        top = jnp.max(rest, axis=1, keepdims=True)
        idx = jnp.min(jnp.where(rest == top, lane, _LANES), axis=1, keepdims=True)
        hit = lane == idx
        picks.append((top, idx, hit))
        total = total + top
        rest = jnp.where(hit, -1.0, rest)
    gates = jnp.zeros_like(probs)
    chosen = jnp.zeros_like(probs)
    for top, _, hit in picks:
        gates = jnp.where(hit, top / total, gates)
        chosen = jnp.where(hit, 1.0, chosen)
    gates_ref[...] = gates
    r_i = lax.broadcasted_iota(jnp.int32, (tm, tm), 0)
    c_i = lax.broadcasted_iota(jnp.int32, (tm, tm), 1)
    tri = jnp.where(r_i > c_i, 1.0, 0.0).astype(_BF)
    before = jnp.dot(tri, chosen.astype(_BF), preferred_element_type=_F32) + run_scr[...]
    meta = jnp.zeros_like(probs)
    for k, (top, idx, hit) in enumerate(picks):
        rank = jnp.sum(jnp.where(hit, before, 0.0), axis=1, keepdims=True)
        meta = jnp.where(lane == _META_EXPERT + k, idx.astype(_F32), meta)
        meta = jnp.where(lane == _META_GATE + k, top / total, meta)
        meta = jnp.where(lane == _META_RANK + k, rank, meta)
    meta_ref[...] = meta
    run_scr[...] += jnp.sum(chosen, axis=0, keepdims=True)
    counts_ref[...] = run_scr[...]


def _router(xb, w_router):
    n, d = xb.shape
    ne = w_router.shape[1]
    assert ne <= _LANES and 3 * TOP_K <= _LANES
    wpad = jnp.pad(w_router, ((0, 0), (0, _LANES - ne))).astype(_BF)
    tm = _pick(n, 512)
    tile = pl.BlockSpec((tm, _LANES), lambda i: (i, 0))
    return pl.pallas_call(
        functools.partial(_router_kernel, n_experts=ne),
        grid=(n // tm,),
        in_specs=[pl.BlockSpec((tm, d), lambda i: (i, 0)), _resident((d, _LANES))],
        out_specs=[tile, tile, pl.BlockSpec((1, _LANES), lambda i: (0, 0))],
        out_shape=[jax.ShapeDtypeStruct((n, _LANES), _F32), jax.ShapeDtypeStruct((n, _LANES), _F32),
                   jax.ShapeDtypeStruct((1, _LANES), _F32)],
        scratch_shapes=[pltpu.VMEM((1, _LANES), _F32)],
        compiler_params=_cparams("arbitrary"),
        name="router",
    )(xb, wpad)


def _route_plan(meta, counts_row, n_experts, tg):
    n = meta.shape[0]
    expert = meta[:, _META_EXPERT:_META_EXPERT + TOP_K].astype(jnp.int32)
    rank = meta[:, _META_RANK:_META_RANK + TOP_K].astype(jnp.int32)
    counts = counts_row[0, :n_experts].astype(jnp.int32)
    padded = (counts + tg - 1) // tg * tg
    ends = jnp.cumsum(padded)
    starts = ends - padded
    pos = (jnp.take(starts, expert) + rank).reshape(-1)
    n_tiles = -(-(TOP_K * n + n_experts * (tg - 1)) // tg)
    tile_start = jnp.arange(n_tiles, dtype=jnp.int32) * tg
    tile_expert = jnp.sum((tile_start[:, None] >= ends[None, :]).astype(jnp.int32), axis=1)
    tile_expert = jnp.minimum(tile_expert, n_experts - 1)
    n_used = (ends[-1:] // tg).astype(jnp.int32)
    return pos, ends, tile_expert, n_used, n_tiles


def _dispatch_kernel(pos_ref, ends_ref, x_ref, out_hbm, sem, *, tm, tg, n_experts):
    base = pl.program_id(0) * (TOP_K * tm)

    def fill_tile(row0):
        for c in range(tg // tm):
            rows = pl.ds(pl.multiple_of(row0 + c * tm, tm), tm)
            fill = pltpu.make_async_copy(x_ref, out_hbm.at[rows], sem)
            fill.start()
            fill.wait()

    @pl.when(pl.program_id(0) == 0)
    def _():
        for e in range(n_experts):
            prev_end = ends_ref[e - 1] if e else 0

            @pl.when(ends_ref[e] > prev_end)
            def _():
                fill_tile(ends_ref[e] - tg)

        for u in range(1, min(n_experts, out_hbm.shape[0] // tg) + 1):
            row0 = out_hbm.shape[0] - u * tg

            @pl.when(row0 >= ends_ref[n_experts - 1])
            def _():
                fill_tile(row0)

    def issue(r, k):
        dst = pos_ref[base + TOP_K * r + k]
        pltpu.make_async_copy(x_ref.at[pl.ds(r, 1)], out_hbm.at[pl.ds(dst, 1)], sem).start()

    _for_each_row(tm, issue)
    for k in range(TOP_K):
        pltpu.make_async_copy(x_ref, out_hbm.at[pl.ds(0, tm)], sem).wait()


def _dispatch(x, pos, ends, n_rows, tg):
    n, d = x.shape
    tm = _pick(n, 256)
    assert tg % tm == 0
    return pl.pallas_call(
        functools.partial(_dispatch_kernel, tm=tm, tg=tg, n_experts=ends.shape[0]),
        grid_spec=pltpu.PrefetchScalarGridSpec(
            num_scalar_prefetch=2,
            grid=(n // tm,),
            in_specs=[pl.BlockSpec((tm, d), lambda i, pos, ends: (i, 0))],
            out_specs=pl.BlockSpec(memory_space=pl.ANY),
            scratch_shapes=[pltpu.SemaphoreType.DMA(())],
        ),
        out_shape=jax.ShapeDtypeStruct((n_rows, d), x.dtype),
        compiler_params=_cparams("arbitrary"),
        name="dispatch",
    )(pos, ends, x)


def _grouped_ffn_kernel(te_ref, nu_ref, x_ref, wi_ref, wo_ref, out_ref, *, chunk):
    del te_ref
    f = wo_ref.shape[1]

    @pl.when(pl.program_id(0) < nu_ref[0])
    def _():
        xb = x_ref[...].astype(_BF)
        for c0 in range(0, f, chunk):
            c1 = min(c0 + chunk, f)
            g = jnp.dot(xb, wi_ref[0, :, c0:c1], preferred_element_type=_F32)
            u = jnp.dot(xb, wi_ref[0, :, f + c0:f + c1], preferred_element_type=_F32)
            h = ((g * _sigmoid(g)) * u).astype(_BF)
            part = jnp.dot(h, wo_ref[0, c0:c1, :], preferred_element_type=_F32)
            if c0 == 0:
                out_ref[...] = part
            else:
                out_ref[...] += part

    @pl.when(pl.program_id(0) >= nu_ref[0])
    def _():
        out_ref[...] = jnp.zeros_like(out_ref)


def _grouped_ffn(xs, w_in, w_out, tile_expert, n_used, tg):
    r, d = xs.shape
    f = w_out.shape[1]
    expert_block = lambda shape: pl.BlockSpec(shape, lambda i, te, nu: (te[i], 0, 0),
                                              pipeline_mode=pl.Buffered(1))
    return pl.pallas_call(
        functools.partial(_grouped_ffn_kernel, chunk=2 * _V7X_MXU_DIM),
        grid_spec=pltpu.PrefetchScalarGridSpec(
            num_scalar_prefetch=2,
            grid=(r // tg,),
            in_specs=[
                pl.BlockSpec((tg, d), lambda i, te, nu: (jnp.minimum(i, nu[0] - 1), 0)),
                expert_block((1, d, 2 * f)),
                expert_block((1, f, d)),
            ],
            out_specs=pl.BlockSpec((tg, d), lambda i, te, nu: (i, 0)),
        ),
        out_shape=jax.ShapeDtypeStruct((r, d), _F32),
        compiler_params=pltpu.CompilerParams(dimension_semantics=("arbitrary",),
                                             vmem_limit_bytes=_VMEM_LIMIT_LARGE),
        name="grouped_ffn",
    )(tile_expert, n_used, xs, w_in, w_out)


def _moe_post_kernel(pos_ref, x_ref, meta_ref, ys_hbm, g_ref, b_ref, p_ref, wg_ref, bg_ref, wp_ref,
                     out_ref, outb_ref, buf, sem, *, alpha, tm, n_tiles):
    i = pl.program_id(0)
    slot = i % 2

    def gather(tile, to_slot):
        base = tile * (TOP_K * tm)

        def issue(r, k):
            src = pos_ref[base + TOP_K * r + k]
            pltpu.make_async_copy(ys_hbm.at[pl.ds(src, 1)], buf.at[to_slot, k, pl.ds(r, 1)],
                                  sem.at[to_slot]).start()

        _for_each_row(tm, issue)

    @pl.when(i == 0)
    def _():
        gather(0, 0)

    @pl.when(i + 1 < n_tiles)
    def _():
        gather(i + 1, 1 - slot)

    for k in range(TOP_K):
        pltpu.make_async_copy(ys_hbm.at[pl.ds(0, tm)], buf.at[slot, k], sem.at[slot]).wait()
    meta = meta_ref[...]
    ff = meta[:, _META_GATE:_META_GATE + 1] * buf[slot, 0]
    for k in range(1, TOP_K):
        ff = ff + meta[:, _META_GATE + k:_META_GATE + k + 1] * buf[slot, k]
    y = _norm_ple(alpha * x_ref[...] + ff, p_ref[...], g_ref, b_ref, wg_ref, bg_ref, wp_ref)
    out_ref[...] = y
    outb_ref[...] = y.astype(_BF)


def _moe_post(x, ys, pos, meta, g, b, p, w_gate, b_gate, w_proj, alpha):
    n, d = x.shape
    pd = p.shape[1]
    tm = _pick(n, 256)
    tile = lambda width: pl.BlockSpec((tm, width), lambda i, pos: (i, 0))
    res = lambda shape: pl.BlockSpec(shape, lambda i, pos: (0,) * len(shape),
                                     pipeline_mode=pl.Buffered(1))
    return pl.pallas_call(
        functools.partial(_moe_post_kernel, alpha=alpha, tm=tm, n_tiles=n // tm),
        grid_spec=pltpu.PrefetchScalarGridSpec(
            num_scalar_prefetch=1,
            grid=(n // tm,),
            in_specs=[tile(d), tile(_LANES), pl.BlockSpec(memory_space=pl.ANY), res((1, d)),
                      res((1, d)), tile(pd), res((d, d)), res((1, d)), res((pd, d))],
            out_specs=[tile(d), tile(d)],
            scratch_shapes=[pltpu.VMEM((2, TOP_K, tm, d), _F32), pltpu.SemaphoreType.DMA((2,))],
        ),
        out_shape=[jax.ShapeDtypeStruct((n, d), _F32), jax.ShapeDtypeStruct((n, d), _BF)],
        compiler_params=_cparams("arbitrary"),
        name="moe_post",
    )(pos, x, meta, ys, g.reshape(1, d), b.reshape(1, d), p, w_gate, b_gate.reshape(1, d), w_proj)


_SPARSE_MOE_MIN_TOKENS = 4096
_GROUP_TILE_ROWS = 512


def _trunk(x, p, lru_conv, lru_h, cm_conv, w):
    bsz, t, d = x.shape
    depth = p.shape[0]
    n = bsz * t
    alpha = (2.0 * depth) ** 0.25
    n_mixers = 2
    x = x.reshape(n, d)
    xb = x
    new_lru_conv, new_lru_h, new_cm_conv = [], [], []
    for i in range(depth):
        j = i // n_mixers
        if i % n_mixers == 0:
            lw = w['rglru_w_out'].shape[1]
            gate, rec = _in_proj(xb, w['rglru_w_in'][j])
            y, buf, hl = _rglru(rec.reshape(bsz, t, lw), gate.reshape(bsz, t, lw), lru_conv[j],
                                lru_h[j], w['rglru_conv_w'][j], w['rglru_conv_b'][j],
                                w['rglru_w_a'][j], w['rglru_b_a'][j], w['rglru_w_x'][j],
                                w['rglru_b_x'][j], w['rglru_lambda'][j])
            new_lru_conv.append(buf)
            new_lru_h.append(hl)
            x, xb = _proj_ln(y.reshape(n, lw), w['rglru_w_out'][j], jnp.zeros((d,), _F32), x,
                             w['ln_mix_g'][i], w['ln_mix_b'][i], alpha)
        else:
            x, xb, buf = _conv_layer(
                xb.astype(_BF).reshape(bsz, t, d), x.reshape(bsz, t, d), cm_conv[j],
                w['cm_w_pw1'][j], w['cm_b_pw1'][j], w['cm_dw_w'][j], w['cm_dw_b'][j],
                w['cm_ln_g'][j], w['cm_ln_b'][j], w['cm_w_pw2'][j], w['cm_b_pw2'][j],
                w['ln_mix_g'][i], w['ln_mix_b'][i], alpha)
            x, xb = x.reshape(n, d), xb.reshape(n, d)
            new_cm_conv.append(buf)
        k = i // 2
        post_args = (w['ln_ffn_g'][i], w['ln_ffn_b'][i], p[i].reshape(n, -1), w['ple_w_gate'][i],
                     w['ple_b_gate'][i], w['ple_w_proj'][i], alpha)
        if i % 2 == 0:
            ff = _ffn(xb, w['ffn_w_in'][k][None], w['ffn_w_out'][k][None])
            x, xb = _post(x, ff, *post_args)
        else:
            gates, meta, counts = _router(xb, w['moe_w_router'][k])
            ne = w['moe_w_router'].shape[-1]
            if n < _SPARSE_MOE_MIN_TOKENS:
                ff = _ffn(xb, w['moe_w_in'][k], w['moe_w_out'][k], gates, tf_pref=1408)
                x, xb = _post(x, ff, *post_args)
            else:
                tg = _GROUP_TILE_ROWS
                pos, ends, tile_expert, n_used, n_tiles = _route_plan(meta, counts, ne, tg)
                xs = _dispatch(x, pos, ends, n_tiles * tg, tg)
                ys = _grouped_ffn(xs, w['moe_w_in'][k], w['moe_w_out'][k], tile_expert, n_used,
                                  tg)
                x, xb = _moe_post(x, ys, pos, meta, *post_args)
    return (x.reshape(bsz, t, d), jnp.stack(new_lru_conv), jnp.stack(new_lru_h),
            jnp.stack(new_cm_conv))


_MATMUL_WEIGHTS = ('rglru_w_in', 'rglru_w_a', 'rglru_w_x', 'rglru_w_out', 'cm_w_pw1', 'cm_w_pw2',
                   'ffn_w_in', 'ffn_w_out', 'moe_w_in', 'moe_w_out', 'ple_w_proj', 'ple_w_gate')


def kernel(x_prompt, x_sample, p_prompt, p_sample, state_rglru_conv, state_rglru_h, state_conv_module, rglru_w_in, rglru_conv_w, rglru_conv_b, rglru_w_a, rglru_b_a, rglru_w_x, rglru_b_x, rglru_lambda, rglru_w_out, cm_w_pw1, cm_b_pw1, cm_dw_w, cm_dw_b, cm_ln_g, cm_ln_b, cm_w_pw2, cm_b_pw2, ffn_w_in, ffn_w_out, moe_w_router, moe_w_in, moe_w_out, ln_mix_g, ln_mix_b, ln_ffn_g, ln_ffn_b, ple_w_proj, ple_w_gate, ple_b_gate):
    w = dict(rglru_w_in=rglru_w_in, rglru_conv_w=rglru_conv_w, rglru_conv_b=rglru_conv_b,
             rglru_w_a=rglru_w_a, rglru_b_a=rglru_b_a, rglru_w_x=rglru_w_x, rglru_b_x=rglru_b_x,
             rglru_lambda=rglru_lambda, rglru_w_out=rglru_w_out, cm_w_pw1=cm_w_pw1,
             cm_b_pw1=cm_b_pw1, cm_dw_w=cm_dw_w, cm_dw_b=cm_dw_b, cm_ln_g=cm_ln_g,
             cm_ln_b=cm_ln_b, cm_w_pw2=cm_w_pw2, cm_b_pw2=cm_b_pw2, ffn_w_in=ffn_w_in,
             ffn_w_out=ffn_w_out, moe_w_router=moe_w_router, moe_w_in=moe_w_in,
             moe_w_out=moe_w_out, ln_mix_g=ln_mix_g, ln_mix_b=ln_mix_b, ln_ffn_g=ln_ffn_g,
             ln_ffn_b=ln_ffn_b, ple_w_proj=ple_w_proj, ple_w_gate=ple_w_gate,
             ple_b_gate=ple_b_gate)
    for name in _MATMUL_WEIGHTS:
        w[name] = w[name].astype(_BF)
    bp = x_prompt.shape[0]
    dt = x_prompt.dtype
    zero_conv = jnp.zeros((state_rglru_conv.shape[0], bp) + state_rglru_conv.shape[2:], dt)
    zero_h = jnp.zeros((state_rglru_h.shape[0], bp) + state_rglru_h.shape[2:], dt)
    zero_cm = jnp.zeros((state_conv_module.shape[0], bp) + state_conv_module.shape[2:], dt)
    y_p, conv_p, h_p, cm_p = _trunk(x_prompt, p_prompt, zero_conv, zero_h, zero_cm, w)
    y_s, conv_s, h_s, cm_s = _trunk(x_sample, p_sample, state_rglru_conv, state_rglru_h,
                                    state_conv_module, w)
    return (y_p, y_s, conv_p, h_p, cm_p, conv_s, h_s, cm_s)
```

```python
import functools

import jax
import jax.numpy as jnp
from jax import lax
from jax.experimental import pallas as pl
from jax.experimental.pallas import tpu as pltpu

_BF = jnp.bfloat16
_F32 = jnp.float32

LRU_C = 8.0
LN_EPS = 1e-5
TOP_K = 2

_V7X_VMEM_BYTES = 64 * 1024 * 1024
_VMEM_LIMIT = _V7X_VMEM_BYTES - 8 * 1024 * 1024
_VMEM_LIMIT_LARGE = _V7X_VMEM_BYTES - 4 * 1024 * 1024
_SUBLANES = 8
_LANES = 128
_V7X_MXU_DIM = 256
_ROW_BLOCK = 128


def _cparams(*sem):
    return pltpu.CompilerParams(dimension_semantics=sem, vmem_limit_bytes=_VMEM_LIMIT)


def _pick(n, pref, mult=_SUBLANES):
    best = None
    for c in range(mult, min(n, pref) + 1, mult):
        if n % c == 0:
            best = c
    assert best is not None, (n, pref, mult)
    return best


def _row_blocks(tm):
    rb = _pick(tm, _ROW_BLOCK)
    return [slice(r0, r0 + rb) for r0 in range(0, tm, rb)]


def _for_each_row(tm, issue):
    def group(g, c):
        r0 = pl.multiple_of(g * _SUBLANES, _SUBLANES)
        for j in range(_SUBLANES):
            for k in range(TOP_K):
                issue(r0 + j, k)
        return c

    lax.fori_loop(0, tm // _SUBLANES, group, 0)


def _resident(shape):
    nd = len(shape)
    return pl.BlockSpec(shape, lambda *_: (0,) * nd, pipeline_mode=pl.Buffered(1))


def _layer_norm(x, g, b):
    mu = jnp.mean(x, axis=-1, keepdims=True)
    xc = x - mu
    var = jnp.mean(xc * xc, axis=-1, keepdims=True)
    return xc * lax.rsqrt(var + LN_EPS) * g + b


def _sigmoid(x):
    return 1.0 / (1.0 + jnp.exp(-x))


def _gelu_tanh(x):
    c = 0.7978845608028654
    return 0.5 * x * (1.0 + jnp.tanh(c * (x + 0.044715 * (x * x * x))))


def _softplus(x):
    return jnp.maximum(x, 0.0) + jnp.log1p(jnp.exp(-jnp.abs(x)))


def _in_proj_kernel(x_ref, wg_ref, wr_ref, gate_ref, rec_ref, xb_scr):
    @pl.when(pl.program_id(1) == 0)
    def _():
        xb_scr[...] = x_ref[...].astype(_BF)

    for rows in _row_blocks(x_ref.shape[0]):
        xb = xb_scr[rows, :]
        gate_ref[rows, :] = _gelu_tanh(jnp.dot(xb, wg_ref[...], preferred_element_type=_F32))
        rec_ref[rows, :] = jnp.dot(xb, wr_ref[...], preferred_element_type=_F32)


def _in_proj(x, w):
    n, d = x.shape
    lw = w.shape[1] // 2
    tm = _pick(n, 1024)
    tn = _pick(lw, 512, _LANES)
    nj = lw // tn
    return pl.pallas_call(
        _in_proj_kernel,
        grid=(n // tm, nj),
        in_specs=[
            pl.BlockSpec((tm, d), lambda i, j: (i, 0)),
            pl.BlockSpec((d, tn), lambda i, j: (0, j)),
            pl.BlockSpec((d, tn), lambda i, j: (0, j + nj)),
        ],
        out_specs=[
            pl.BlockSpec((tm, tn), lambda i, j: (i, j)),
            pl.BlockSpec((tm, tn), lambda i, j: (i, j)),
        ],
        out_shape=[jax.ShapeDtypeStruct((n, lw), _F32)] * 2,
        scratch_shapes=[pltpu.VMEM((tm, d), _BF)],
        compiler_params=_cparams("parallel", "arbitrary"),
        name="in_proj",
    )(x, w, w)


def _rglru_kernel(rec_ref, gate_ref, buf0_ref, h0_ref, cw_ref, cb_ref, wa_ref, ba_ref, wx_ref,
                  bx_ref, lam_ref, y_ref, nbuf_ref, hlast_ref, xp_scr, a_scr, b_scr, carry_scr,
                  *, tc, heads, kw):
    t = pl.program_id(1)
    lw = rec_ref.shape[-1]
    blk = lw // heads
    hist = _SUBLANES
    base = hist - (kw - 1)

    @pl.when(t == 0)
    def _():
        xp_scr[0:hist, :] = buf0_ref[0]
        carry_scr[...] = jnp.broadcast_to(h0_ref[0], (_SUBLANES, lw))

    xp_scr[hist:hist + tc, :] = rec_ref[0]
    decay = -LRU_C * _softplus(-lam_ref[...])
    for h in range(heads):
        sl = slice(h * blk, (h + 1) * blk)
        whole = xp_scr[:, sl]
        xc = jnp.broadcast_to(cb_ref[:, sl], (tc, blk))
        for k in range(kw):
            j = (base + k) % _SUBLANES
            q0 = (base + k) - j
            win = whole if j == 0 else pltpu.roll(whole, hist + tc - j, 0)
            xc = xc + cw_ref[k:k + 1, sl] * win[q0:q0 + tc]
        xcb = xc.astype(_BF)
        r = _sigmoid(jnp.dot(xcb, wa_ref[h], preferred_element_type=_F32) + ba_ref[:, sl])
        ig = _sigmoid(jnp.dot(xcb, wx_ref[h], preferred_element_type=_F32) + bx_ref[:, sl])
        log_a = r * decay[:, sl]
        a = jnp.exp(log_a)
        a_scr[:, sl] = a
        b_scr[:, sl] = jnp.sqrt(-jnp.tanh(log_a) * (a * a + 1.0)) * (ig * xc)

    row = lax.broadcasted_iota(jnp.int32, (_SUBLANES, lw), 0)

    def body(g, carry):
        r0 = pl.multiple_of(g * _SUBLANES, _SUBLANES)
        a = a_scr[pl.ds(r0, _SUBLANES), :]
        b = b_scr[pl.ds(r0, _SUBLANES), :]
        for s in (1, 2, 4):
            m = row >= s
            a_sh = jnp.where(m, pltpu.roll(a, s, 0), 1.0)
            b_sh = jnp.where(m, pltpu.roll(b, s, 0), 0.0)
            b = a * b_sh + b
            a = a * a_sh
        hrows = a * carry + b
        b_scr[pl.ds(r0, _SUBLANES), :] = hrows
        return jnp.broadcast_to(hrows[_SUBLANES - 1:_SUBLANES, :], (_SUBLANES, lw))

    carry = lax.fori_loop(0, tc // _SUBLANES, body, carry_scr[...])
    carry_scr[...] = carry
    y_ref[0] = (b_scr[...] * gate_ref[0]).astype(y_ref.dtype)
    hlast_ref[0] = carry
    tail = xp_scr[tc:tc + hist, :]
    nbuf_ref[0] = tail
    xp_scr[0:hist, :] = tail


def _rglru_layer_kernel(rec_ref, gate_ref, buf0_ref, h0_ref, cw_ref, cb_ref, wa_ref, ba_ref,
                        wx_ref, bx_ref, lam_ref, x_ref, wo_ref, g_ref, b_ref, out_ref, outb_ref,
                        nbuf_ref, hlast_ref, xp_scr, a_scr, b_scr, carry_scr, y_scr, mix_scr,
                        *, tc, heads, kw, n_chunks, n_live, alpha):
    s = pl.program_id(0)
    t = s % n_chunks
    live = s < n_live
    lw = rec_ref.shape[-1]
    blk = lw // heads
    hist = _SUBLANES
    base = hist - (kw - 1)

    @pl.when(s == 0)
    def _():
        y_scr[...] = jnp.zeros_like(y_scr)

    @pl.when((t == 0) & live)
    def _():
        xp_scr[0:hist, :] = buf0_ref[0]
        carry_scr[...] = jnp.broadcast_to(h0_ref[0], (_SUBLANES, lw))

    xp_scr[hist:hist + tc, :] = rec_ref[0]
    d = out_ref.shape[-1]
    dblk = d // heads
    y_prev = y_scr[...]
    decay = -LRU_C * _softplus(-lam_ref[...])
    for h in range(heads):
        cols = slice(h * dblk, (h + 1) * dblk)
        mix_scr[:, cols] = jnp.dot(y_prev, wo_ref[:, cols], preferred_element_type=_F32)
        sl = slice(h * blk, (h + 1) * blk)
        whole = xp_scr[:, sl]
        xc = jnp.broadcast_to(cb_ref[:, sl], (tc, blk))
        for k in range(kw):
            j = (base + k) % _SUBLANES
            q0 = (base + k) - j
            win = whole if j == 0 else pltpu.roll(whole, hist + tc - j, 0)
            xc = xc + cw_ref[k:k + 1, sl] * win[q0:q0 + tc]
        xcb = xc.astype(_BF)
        r = _sigmoid(jnp.dot(xcb, wa_ref[h], preferred_element_type=_F32) + ba_ref[:, sl])
        ig = _sigmoid(jnp.dot(xcb, wx_ref[h], preferred_element_type=_F32) + bx_ref[:, sl])
        log_a = r * decay[:, sl]
        a = jnp.exp(log_a)
        a_scr[:, sl] = a
        b_scr[:, sl] = jnp.sqrt(-jnp.tanh(log_a) * (a * a + 1.0)) * (ig * xc)
    xo = _layer_norm(alpha * x_ref[0] + mix_scr[...], g_ref[...], b_ref[...])
    out_ref[0] = xo
    outb_ref[0] = xo.astype(_BF)

    row = lax.broadcasted_iota(jnp.int32, (_SUBLANES, lw), 0)

    def body(g, carry):
        r0 = pl.multiple_of(g * _SUBLANES, _SUBLANES)
        a = a_scr[pl.ds(r0, _SUBLANES), :]
        b = b_scr[pl.ds(r0, _SUBLANES), :]
        for sh in (1, 2, 4):
            m = row >= sh
            a_sh = jnp.where(m, pltpu.roll(a, sh, 0), 1.0)
            b_sh = jnp.where(m, pltpu.roll(b, sh, 0), 0.0)
            b = a * b_sh + b
            a = a * a_sh
        hrows = a * carry + b
        b_scr[pl.ds(r0, _SUBLANES), :] = hrows
        return jnp.broadcast_to(hrows[_SUBLANES - 1:_SUBLANES, :], (_SUBLANES, lw))

    carry = lax.fori_loop(0, tc // _SUBLANES, body, carry_scr[...])
    y_scr[...] = (b_scr[...] * gate_ref[0]).astype(_BF)

    @pl.when(live)
    def _():
        carry_scr[...] = carry
        hlast_ref[0] = carry
        tail = xp_scr[tc:tc + hist, :]
        nbuf_ref[0] = tail
        xp_scr[0:hist, :] = tail


def _rglru_layer(rec, gate, x, buf0, h0, conv_w, conv_b, w_a, b_a, w_x, b_x, lam, w_out, ln_g,
                 ln_b, alpha):
    bsz, t, lw = rec.shape
    d = x.shape[-1]
    heads, blk, _ = w_a.shape
    kw = conv_w.shape[0]
    assert kw - 1 <= _SUBLANES <= t and t % _SUBLANES == 0
    tc = _pick(t, 256)
    n_chunks = t // tc
    n_live = bsz * n_chunks
    buf_pad = jnp.pad(buf0, ((0, 0), (_SUBLANES - (kw - 1), 0), (0, 0)))
    row = lambda v: v.reshape(1, -1)

    def cur(s):
        c = jnp.minimum(s, n_live - 1)
        return c // n_chunks, c % n_chunks

    def prev(s):
        c = jnp.maximum(s - 1, 0)
        return c // n_chunks, c % n_chunks

    cur_seq = pl.BlockSpec((1, tc, lw), lambda s: (*cur(s), 0))
    prev_seq = pl.BlockSpec((1, tc, d), lambda s: (*prev(s), 0))
    per_b = pl.BlockSpec((1, _SUBLANES, lw), lambda s: (cur(s)[0], 0, 0))
    out, outb, nbuf, hlast = pl.pallas_call(
        functools.partial(_rglru_layer_kernel, tc=tc, heads=heads, kw=kw, n_chunks=n_chunks,
                          n_live=n_live, alpha=alpha),
        grid=(n_live + 1,),
        in_specs=[
            cur_seq, cur_seq, per_b,
            pl.BlockSpec((1, 1, lw), lambda s: (cur(s)[0], 0, 0)),
            _resident((kw, lw)), _resident((1, lw)),
            _resident((heads, blk, blk)), _resident((1, lw)),
            _resident((heads, blk, blk)), _resident((1, lw)),
            _resident((1, lw)),
            prev_seq, _resident((lw, d)), _resident((1, d)), _resident((1, d)),
        ],
        out_specs=[prev_seq, prev_seq, per_b, per_b],
        out_shape=[
            jax.ShapeDtypeStruct((bsz, t, d), _F32),
            jax.ShapeDtypeStruct((bsz, t, d), _BF),
            jax.ShapeDtypeStruct((bsz, _SUBLANES, lw), _F32),
            jax.ShapeDtypeStruct((bsz, _SUBLANES, lw), _F32),
        ],
        scratch_shapes=[
            pltpu.VMEM((_SUBLANES + tc, lw), _F32),
            pltpu.VMEM((tc, lw), _F32),
            pltpu.VMEM((tc, lw), _F32),
            pltpu.VMEM((_SUBLANES, lw), _F32),
            pltpu.VMEM((tc, lw), _BF),
            pltpu.VMEM((tc, d), _F32),
        ],
        compiler_params=_cparams("arbitrary"),
        name="rglru_layer",
    )(rec, gate, buf_pad, h0.reshape(bsz, 1, lw), conv_w, row(conv_b), w_a, row(b_a), w_x,
      row(b_x), row(lam), x, w_out, row(ln_g), row(ln_b))
    return out, outb, nbuf[:, _SUBLANES - (kw - 1):], hlast[:, 0]


def _rglru(rec, gate, buf0, h0, conv_w, conv_b, w_a, b_a, w_x, b_x, lam):
    bsz, t, lw = rec.shape
    heads, blk, _ = w_a.shape
    kw = conv_w.shape[0]
    assert kw - 1 <= _SUBLANES <= t and t % _SUBLANES == 0
    tc = _pick(t, 256)
    buf_pad = jnp.pad(buf0, ((0, 0), (_SUBLANES - (kw - 1), 0), (0, 0)))
    row = lambda v: v.reshape(1, lw)
    seq = pl.BlockSpec((1, tc, lw), lambda b, i: (b, i, 0))
    per_b = pl.BlockSpec((1, _SUBLANES, lw), lambda b, i: (b, 0, 0))
    y, nbuf, hlast = pl.pallas_call(
        functools.partial(_rglru_kernel, tc=tc, heads=heads, kw=kw),
        grid=(bsz, t // tc),
        in_specs=[
            seq, seq, per_b,
            pl.BlockSpec((1, 1, lw), lambda b, i: (b, 0, 0)),
            _resident((kw, lw)), _resident((1, lw)),
            _resident((heads, blk, blk)), _resident((1, lw)),
            _resident((heads, blk, blk)), _resident((1, lw)),
            _resident((1, lw)),
        ],
        out_specs=[seq, per_b, per_b],
        out_shape=[
            jax.ShapeDtypeStruct((bsz, t, lw), _BF),
            jax.ShapeDtypeStruct((bsz, _SUBLANES, lw), _F32),
            jax.ShapeDtypeStruct((bsz, _SUBLANES, lw), _F32),
        ],
        scratch_shapes=[
            pltpu.VMEM((_SUBLANES + tc, lw), _F32),
            pltpu.VMEM((tc, lw), _F32),
            pltpu.VMEM((tc, lw), _F32),
            pltpu.VMEM((_SUBLANES, lw), _F32),
        ],
        compiler_params=_cparams("parallel", "arbitrary"),
        name="rglru",
    )(rec, gate, buf_pad, h0.reshape(bsz, 1, lw), conv_w, row(conv_b), w_a, row(b_a), w_x,
      row(b_x), row(lam))
    return y, nbuf[:, _SUBLANES - (kw - 1):], hlast[:, 0]


def _proj_ln_kernel(a_ref, w_ref, bias_ref, x_ref, g_ref, b_ref, out_ref, outb_ref, *, alpha):
    for rows in _row_blocks(a_ref.shape[0]):
        mix = jnp.dot(a_ref[rows, :], w_ref[...], preferred_element_type=_F32) + bias_ref[...]
        y = _layer_norm(alpha * x_ref[rows, :] + mix, g_ref[...], b_ref[...])
        out_ref[rows, :] = y
        outb_ref[rows, :] = y.astype(_BF)


def _proj_ln(a, w, bias, x, g, b, alpha):
    n, k = a.shape
    d = w.shape[1]
    tm = _pick(n, 512)
    tile = lambda width: pl.BlockSpec((tm, width), lambda i: (i, 0))
    return pl.pallas_call(
        functools.partial(_proj_ln_kernel, alpha=alpha),
        grid=(n // tm,),
        in_specs=[tile(k), _resident((k, d)), _resident((1, d)), tile(d), _resident((1, d)),
                  _resident((1, d))],
        out_specs=[tile(d), tile(d)],
        out_shape=[jax.ShapeDtypeStruct((n, d), _F32), jax.ShapeDtypeStruct((n, d), _BF)],
        compiler_params=_cparams("parallel"),
        name="proj_ln",
    )(a, w, bias.reshape(1, d), x, g.reshape(1, d), b.reshape(1, d))


def _ffn_kernel(x_ref, wg_ref, wu_ref, wo_ref, *rest, gated):
    if gated:
        gates_ref, out_ref = rest
    else:
        (out_ref,) = rest
    e = pl.program_id(1)
    f = pl.program_id(2)

    @pl.when((e == 0) & (f == 0))
    def _():
        out_ref[...] = jnp.zeros_like(out_ref)

    xb = x_ref[...]
    g = jnp.dot(xb, wg_ref[0], preferred_element_type=_F32)
    u = jnp.dot(xb, wu_ref[0], preferred_element_type=_F32)
    h = (g * _sigmoid(g)) * u
    if gated:
        lane = lax.broadcasted_iota(jnp.int32, gates_ref.shape, 1)
        h = h * jnp.sum(jnp.where(lane == e, gates_ref[...], 0.0), axis=1, keepdims=True)
    out_ref[...] += jnp.dot(h.astype(_BF), wo_ref[0], preferred_element_type=_F32)


def _ffn(xb, w_in, w_out, gates=None, tf_pref=512):
    n, d = xb.shape
    ne, f, _ = w_out.shape
    tm = _pick(n, 1024)
    tf = _pick(f, tf_pref, _LANES)
    nf = f // tf
    in_specs = [
        pl.BlockSpec((tm, d), lambda i, e, j: (i, 0)),
        pl.BlockSpec((1, d, tf), lambda i, e, j: (e, 0, j)),
        pl.BlockSpec((1, d, tf), lambda i, e, j: (e, 0, j + nf)),
        pl.BlockSpec((1, tf, d), lambda i, e, j: (e, j, 0)),
    ]
    args = [xb, w_in, w_in, w_out]
    if gates is not None:
        in_specs.append(pl.BlockSpec((tm, _LANES), lambda i, e, j: (i, 0)))
        args.append(gates)
    return pl.pallas_call(
        functools.partial(_ffn_kernel, gated=gates is not None),
        grid=(n // tm, ne, nf),
        in_specs=in_specs,
        out_specs=pl.BlockSpec((tm, d), lambda i, e, j: (i, 0)),
        out_shape=jax.ShapeDtypeStruct((n, d), _F32),
        compiler_params=_cparams("parallel", "arbitrary", "arbitrary"),
        name="ffn",
    )(*args)


def _norm_ple(pre, p, g_ref, b_ref, wg_ref, bg_ref, wp_ref):
    xn = _layer_norm(pre, g_ref[...], b_ref[...])
    gate = _sigmoid(jnp.dot(xn.astype(_BF), wg_ref[...], preferred_element_type=_F32)
                    + bg_ref[...])
    proj = jnp.dot(p.astype(_BF), wp_ref[...], preferred_element_type=_F32)
    return xn + gate * proj


def _post_kernel(x_ref, ff_ref, g_ref, b_ref, p_ref, wg_ref, bg_ref, wp_ref, out_ref, outb_ref,
                 *, alpha):
    y = _norm_ple(alpha * x_ref[...] + ff_ref[...], p_ref[...], g_ref, b_ref, wg_ref, bg_ref,
                  wp_ref)
    out_ref[...] = y
    outb_ref[...] = y.astype(_BF)


def _post(x, ff, g, b, p, w_gate, b_gate, w_proj, alpha):
    n, d = x.shape
    pd = p.shape[1]
    tm = _pick(n, 512)
    tile = lambda width: pl.BlockSpec((tm, width), lambda i: (i, 0))
    return pl.pallas_call(
        functools.partial(_post_kernel, alpha=alpha),
        grid=(n // tm,),
        in_specs=[tile(d), tile(d), _resident((1, d)), _resident((1, d)), tile(pd),
                  _resident((d, d)), _resident((1, d)), _resident((pd, d))],
        out_specs=[tile(d), tile(d)],
        out_shape=[jax.ShapeDtypeStruct((n, d), _F32), jax.ShapeDtypeStruct((n, d), _BF)],
        compiler_params=_cparams("parallel"),
        name="post",
    )(x, ff, g.reshape(1, d), b.reshape(1, d), p, w_gate, b_gate.reshape(1, d), w_proj)


def _pw1_glu_kernel(x_ref, wv_ref, wg_ref, bv_ref, bg_ref, v_ref):
    for rows in _row_blocks(x_ref.shape[0]):
        xb = x_ref[rows, :]
        val = jnp.dot(xb, wv_ref[...], preferred_element_type=_F32) + bv_ref[...]
        gate = jnp.dot(xb, wg_ref[...], preferred_element_type=_F32) + bg_ref[...]
        v_ref[rows, :] = val * _sigmoid(gate)


def _pw1_glu(xb, w, bias):
    n, d = xb.shape
    dv = w.shape[1] // 2
    tm = _pick(n, 1024)
    tn = _pick(dv, 512, _LANES)
    nj = dv // tn
    bias = bias.reshape(1, 2 * dv)
    return pl.pallas_call(
        _pw1_glu_kernel,
        grid=(n // tm, nj),
        in_specs=[
            pl.BlockSpec((tm, d), lambda i, j: (i, 0)),
            pl.BlockSpec((d, tn), lambda i, j: (0, j)),
            pl.BlockSpec((d, tn), lambda i, j: (0, j + nj)),
            pl.BlockSpec((1, tn), lambda i, j: (0, j)),
            pl.BlockSpec((1, tn), lambda i, j: (0, j + nj)),
        ],
        out_specs=pl.BlockSpec((tm, tn), lambda i, j: (i, j)),
        out_shape=jax.ShapeDtypeStruct((n, dv), _F32),
        compiler_params=_cparams("parallel", "arbitrary"),
        name="pw1_glu",
    )(xb, w, w, bias, bias)


def _convmod_kernel(v_ref, buf0_ref, w_ref, b_ref, g_ref, beta_ref, y_ref, nbuf_ref, vp_scr,
                    c_scr, *, tc, ksz, hist):
    t = pl.program_id(1)
    d = v_ref.shape[-1]
    base = hist - (ksz - 1)

    @pl.when(t == 0)
    def _():
        vp_scr[0:hist, :] = buf0_ref[0]

    vp_scr[hist:hist + tc, :] = v_ref[0]
    _depthwise_conv(vp_scr, 0, tc, hist, w_ref, b_ref, ksz, c_scr)
    z = _layer_norm(c_scr[...], g_ref[...], beta_ref[...])
    y_ref[0] = (z * _sigmoid(z)).astype(y_ref.dtype)
    tail = vp_scr[tc:tc + hist, :]
    nbuf_ref[0] = tail
    vp_scr[0:hist, :] = tail


def _depthwise_conv(vp_scr, r0, nrows, hist, w_ref, b_ref, ksz, c_scr):
    d = vp_scr.shape[-1]
    base = hist - (ksz - 1)
    rows = hist + nrows
    for c in range(d // _LANES):
        sl = slice(c * _LANES, (c + 1) * _LANES)
        acc = jnp.broadcast_to(b_ref[:, sl], (nrows, _LANES))
        whole = vp_scr[r0:r0 + rows, sl]
        for j in range(_SUBLANES):
            taps = [k for k in range(ksz) if (base + k) % _SUBLANES == j]
            if not taps:
                continue
            win = whole if j == 0 else pltpu.roll(whole, rows - j, 0)
            for k in taps:
                q0 = ((base + k) // _SUBLANES) * _SUBLANES
                acc = acc + w_ref[k:k + 1, sl] * win[q0:q0 + nrows]
        c_scr[r0:r0 + nrows, sl] = acc


def _conv_layer_kernel(xb_ref, x_ref, buf0_ref, w1_ref, b1_ref, dw_ref, db_ref, cg_ref, cb_ref,
                       w2_ref, b2_ref, g_ref, b_ref, out_ref, outb_ref, nbuf_ref, vp_scr, c_scr,
                       *, tc, sub, ksz, hist, alpha):
    d = x_ref.shape[-1]

    @pl.when(pl.program_id(1) == 0)
    def _():
        vp_scr[0:hist, :] = buf0_ref[0]

    for r0 in range(0, tc, sub):
        rows = slice(r0, r0 + sub)
        xb = xb_ref[0, rows, :]
        val = jnp.dot(xb, w1_ref[:, :d], preferred_element_type=_F32) + b1_ref[:, :d]
        gate = jnp.dot(xb, w1_ref[:, d:], preferred_element_type=_F32) + b1_ref[:, d:]
        vp_scr[hist + r0:hist + r0 + sub, :] = val * _sigmoid(gate)
        _depthwise_conv(vp_scr, r0, sub, hist, dw_ref, db_ref, ksz, c_scr)
        z = _layer_norm(c_scr[rows, :], cg_ref[...], cb_ref[...])
        y = (z * _sigmoid(z)).astype(_BF)
        mix = jnp.dot(y, w2_ref[...], preferred_element_type=_F32) + b2_ref[...]
        xo = _layer_norm(alpha * x_ref[0, rows, :] + mix, g_ref[...], b_ref[...])
        out_ref[0, rows, :] = xo
        outb_ref[0, rows, :] = xo.astype(_BF)
    tail = vp_scr[tc:tc + hist, :]
    nbuf_ref[0] = tail
    vp_scr[0:hist, :] = tail


def _conv_layer(xb, x, buf0, w_pw1, b_pw1, dw_w, dw_b, cm_g, cm_b, w_pw2, b_pw2, ln_g, ln_b,
                alpha):
    bsz, t, d = x.shape
    ksz = dw_w.shape[0]
    hist = -(-(ksz - 1) // _SUBLANES) * _SUBLANES
    assert t % _SUBLANES == 0 and d % _LANES == 0
    tc = _pick(t, 256)
    sub = _pick(tc, 128)
    buf_pad = jnp.pad(buf0, ((0, 0), (hist - (ksz - 1), 0), (0, 0)))
    row = lambda a: a.reshape(1, -1)
    seq = pl.BlockSpec((1, tc, d), lambda b, i: (b, i, 0))
    per_b = pl.BlockSpec((1, hist, d), lambda b, i: (b, 0, 0))
    out, outb, nbuf = pl.pallas_call(
        functools.partial(_conv_layer_kernel, tc=tc, sub=sub, ksz=ksz, hist=hist, alpha=alpha),
        grid=(bsz, t // tc),
        in_specs=[seq, seq, per_b, _resident((d, 2 * d)), _resident((1, 2 * d)),
                  _resident((ksz, d)), _resident((1, d)), _resident((1, d)), _resident((1, d)),
                  _resident((d, d)), _resident((1, d)), _resident((1, d)), _resident((1, d))],
        out_specs=[seq, seq, per_b],
        out_shape=[jax.ShapeDtypeStruct((bsz, t, d), _F32), jax.ShapeDtypeStruct((bsz, t, d), _BF),
                   jax.ShapeDtypeStruct((bsz, hist, d), _F32)],
        scratch_shapes=[pltpu.VMEM((hist + tc, d), _F32), pltpu.VMEM((tc, d), _F32)],
        compiler_params=_cparams("parallel", "arbitrary"),
        name="conv_layer",
    )(xb, x, buf_pad, w_pw1, row(b_pw1), dw_w, row(dw_b), row(cm_g), row(cm_b), w_pw2,
      row(b_pw2), row(ln_g), row(ln_b))
    return out, outb, nbuf[:, hist - (ksz - 1):]


def _convmod(v, buf0, dw_w, dw_b, ln_g, ln_b):
    bsz, t, d = v.shape
    ksz = dw_w.shape[0]
    hist = -(-(ksz - 1) // _SUBLANES) * _SUBLANES
    assert t % _SUBLANES == 0 and d % _LANES == 0
    tc = _pick(t, 128)
    buf_pad = jnp.pad(buf0, ((0, 0), (hist - (ksz - 1), 0), (0, 0)))
    row = lambda a: a.reshape(1, d)
    seq = pl.BlockSpec((1, tc, d), lambda b, i: (b, i, 0))
    per_b = pl.BlockSpec((1, hist, d), lambda b, i: (b, 0, 0))
    y, nbuf = pl.pallas_call(
        functools.partial(_convmod_kernel, tc=tc, ksz=ksz, hist=hist),
        grid=(bsz, t // tc),
        in_specs=[seq, per_b, _resident((ksz, d)), _resident((1, d)), _resident((1, d)),
                  _resident((1, d))],
        out_specs=[seq, per_b],
        out_shape=[jax.ShapeDtypeStruct((bsz, t, d), _BF),
                   jax.ShapeDtypeStruct((bsz, hist, d), _F32)],
        scratch_shapes=[pltpu.VMEM((hist + tc, d), _F32), pltpu.VMEM((tc, d), _F32)],
        compiler_params=_cparams("parallel", "arbitrary"),
        name="convmod",
    )(v, buf_pad, dw_w, row(dw_b), row(ln_g), row(ln_b))
    return y, nbuf[:, hist - (ksz - 1):]


_META_EXPERT, _META_GATE, _META_RANK = 0, TOP_K, 2 * TOP_K


def _router_kernel(xb_ref, w_ref, gates_ref, meta_ref, counts_ref, run_scr, *, n_experts):
    @pl.when(pl.program_id(0) == 0)
    def _():
        run_scr[...] = jnp.zeros_like(run_scr)

    logits = jnp.dot(xb_ref[...], w_ref[...], preferred_element_type=_F32)
    tm = logits.shape[0]
    lane = lax.broadcasted_iota(jnp.int32, logits.shape, 1)
    valid = lane < n_experts
    logits = jnp.where(valid, logits, -1e30)
    ex = jnp.where(valid, jnp.exp(logits - jnp.max(logits, axis=1, keepdims=True)), 0.0)
    probs = ex / jnp.sum(ex, axis=1, keepdims=True)
    rest = jnp.where(valid, probs, -1.0)
    total = jnp.zeros_like(probs[:, :1])
    picks = []
    for _ in range(TOP_K):
        top = jnp.max(rest, axis=1, keepdims=True)
        idx = jnp.min(jnp.where(rest == top, lane, _LANES), axis=1, keepdims=True)
        hit = lane == idx
        picks.append((top, idx, hit))
        total = total + top
        rest = jnp.where(hit, -1.0, rest)
    gates = jnp.zeros_like(probs)
    chosen = jnp.zeros_like(probs)
    for top, _, hit in picks:
        gates = jnp.where(hit, top / total, gates)
        chosen = jnp.where(hit, 1.0, chosen)
    gates_ref[...] = gates
    r_i = lax.broadcasted_iota(jnp.int32, (tm, tm), 0)
    c_i = lax.broadcasted_iota(jnp.int32, (tm, tm), 1)
    tri = jnp.where(r_i > c_i, 1.0, 0.0).astype(_BF)
    before = jnp.dot(tri, chosen.astype(_BF), preferred_element_type=_F32) + run_scr[...]
    meta = jnp.zeros_like(probs)
    for k, (top, idx, hit) in enumerate(picks):
        rank = jnp.sum(jnp.where(hit, before, 0.0), axis=1, keepdims=True)
        meta = jnp.where(lane == _META_EXPERT + k, idx.astype(_F32), meta)
        meta = jnp.where(lane == _META_GATE + k, top / total, meta)
        meta = jnp.where(lane == _META_RANK + k, rank, meta)
    meta_ref[...] = meta
    run_scr[...] += jnp.sum(chosen, axis=0, keepdims=True)
    counts_ref[...] = run_scr[...]


def _router(xb, w_router):
    n, d = xb.shape
    ne = w_router.shape[1]
    assert ne <= _LANES and 3 * TOP_K <= _LANES
    wpad = jnp.pad(w_router, ((0, 0), (0, _LANES - ne))).astype(_BF)
    tm = _pick(n, 512)
    tile = pl.BlockSpec((tm, _LANES), lambda i: (i, 0))
    return pl.pallas_call(
        functools.partial(_router_kernel, n_experts=ne),
        grid=(n // tm,),
        in_specs=[pl.BlockSpec((tm, d), lambda i: (i, 0)), _resident((d, _LANES))],
        out_specs=[tile, tile, pl.BlockSpec((1, _LANES), lambda i: (0, 0))],
        out_shape=[jax.ShapeDtypeStruct((n, _LANES), _F32), jax.ShapeDtypeStruct((n, _LANES), _F32),
                   jax.ShapeDtypeStruct((1, _LANES), _F32)],
        scratch_shapes=[pltpu.VMEM((1, _LANES), _F32)],
        compiler_params=_cparams("arbitrary"),
        name="router",
    )(xb, wpad)


def _route_plan(meta, counts_row, n_experts, tg):
    n = meta.shape[0]
    expert = meta[:, _META_EXPERT:_META_EXPERT + TOP_K].astype(jnp.int32)
    rank = meta[:, _META_RANK:_META_RANK + TOP_K].astype(jnp.int32)
    counts = counts_row[0, :n_experts].astype(jnp.int32)
    padded = (counts + tg - 1) // tg * tg
    ends = jnp.cumsum(padded)
    starts = ends - padded
    pos = (jnp.take(starts, expert) + rank).reshape(-1)
    n_tiles = -(-(TOP_K * n + n_experts * (tg - 1)) // tg)
    tile_start = jnp.arange(n_tiles, dtype=jnp.int32) * tg
    tile_expert = jnp.sum((tile_start[:, None] >= ends[None, :]).astype(jnp.int32), axis=1)
    tile_expert = jnp.minimum(tile_expert, n_experts - 1)
    n_used = (ends[-1:] // tg).astype(jnp.int32)
    return pos, ends, tile_expert, n_used, n_tiles


def _dispatch_kernel(pos_ref, ends_ref, x_ref, out_hbm, sem, *, tm, tg, n_experts):
    base = pl.program_id(0) * (TOP_K * tm)

    def fill_tile(row0):
        for c in range(tg // tm):
            rows = pl.ds(pl.multiple_of(row0 + c * tm, tm), tm)
            fill = pltpu.make_async_copy(x_ref, out_hbm.at[rows], sem)
            fill.start()
            fill.wait()

    @pl.when(pl.program_id(0) == 0)
    def _():
        for e in range(n_experts):
            prev_end = ends_ref[e - 1] if e else 0

            @pl.when(ends_ref[e] > prev_end)
            def _():
                fill_tile(ends_ref[e] - tg)

        for u in range(1, min(n_experts, out_hbm.shape[0] // tg) + 1):
            row0 = out_hbm.shape[0] - u * tg

            @pl.when(row0 >= ends_ref[n_experts - 1])
            def _():
                fill_tile(row0)

    def issue(r, k):
        dst = pos_ref[base + TOP_K * r + k]
        pltpu.make_async_copy(x_ref.at[pl.ds(r, 1)], out_hbm.at[pl.ds(dst, 1)], sem).start()

    _for_each_row(tm, issue)
    for k in range(TOP_K):
        pltpu.make_async_copy(x_ref, out_hbm.at[pl.ds(0, tm)], sem).wait()


def _dispatch(x, pos, ends, n_rows, tg):
    n, d = x.shape
    tm = _pick(n, 256)
    assert tg % tm == 0
    return pl.pallas_call(
        functools.partial(_dispatch_kernel, tm=tm, tg=tg, n_experts=ends.shape[0]),
        grid_spec=pltpu.PrefetchScalarGridSpec(
            num_scalar_prefetch=2,
            grid=(n // tm,),
            in_specs=[pl.BlockSpec((tm, d), lambda i, pos, ends: (i, 0))],
            out_specs=pl.BlockSpec(memory_space=pl.ANY),
            scratch_shapes=[pltpu.SemaphoreType.DMA(())],
        ),
        out_shape=jax.ShapeDtypeStruct((n_rows, d), x.dtype),
        compiler_params=_cparams("arbitrary"),
        name="dispatch",
    )(pos, ends, x)


def _grouped_ffn_kernel(te_ref, nu_ref, x_ref, wi_ref, wo_ref, out_ref, *, chunk):
    del te_ref
    f = wo_ref.shape[1]

    @pl.when(pl.program_id(0) < nu_ref[0])
    def _():
        xb = x_ref[...].astype(_BF)
        for c0 in range(0, f, chunk):
            c1 = min(c0 + chunk, f)
            g = jnp.dot(xb, wi_ref[0, :, c0:c1], preferred_element_type=_F32)
            u = jnp.dot(xb, wi_ref[0, :, f + c0:f + c1], preferred_element_type=_F32)
            h = ((g * _sigmoid(g)) * u).astype(_BF)
            part = jnp.dot(h, wo_ref[0, c0:c1, :], preferred_element_type=_F32)
            if c0 == 0:
                out_ref[...] = part
            else:
                out_ref[...] += part

    @pl.when(pl.program_id(0) >= nu_ref[0])
    def _():
        out_ref[...] = jnp.zeros_like(out_ref)


def _grouped_ffn(xs, w_in, w_out, tile_expert, n_used, tg):
    r, d = xs.shape
    f = w_out.shape[1]
    expert_block = lambda shape: pl.BlockSpec(shape, lambda i, te, nu: (te[i], 0, 0),
                                              pipeline_mode=pl.Buffered(1))
    return pl.pallas_call(
        functools.partial(_grouped_ffn_kernel, chunk=2 * _V7X_MXU_DIM),
        grid_spec=pltpu.PrefetchScalarGridSpec(
            num_scalar_prefetch=2,
            grid=(r // tg,),
            in_specs=[
                pl.BlockSpec((tg, d), lambda i, te, nu: (jnp.minimum(i, nu[0] - 1), 0)),
                expert_block((1, d, 2 * f)),
                expert_block((1, f, d)),
            ],
            out_specs=pl.BlockSpec((tg, d), lambda i, te, nu: (i, 0)),
        ),
        out_shape=jax.ShapeDtypeStruct((r, d), _F32),
        compiler_params=pltpu.CompilerParams(dimension_semantics=("arbitrary",),
                                             vmem_limit_bytes=_VMEM_LIMIT_LARGE),
        name="grouped_ffn",
    )(tile_expert, n_used, xs, w_in, w_out)


def _moe_post_kernel(pos_ref, x_ref, meta_ref, ys_hbm, g_ref, b_ref, p_ref, wg_ref, bg_ref, wp_ref,
                     out_ref, outb_ref, buf, sem, *, alpha, tm, n_tiles):
    i = pl.program_id(0)
    slot = i % 2

    def gather(tile, to_slot):
        base = tile * (TOP_K * tm)

        def issue(r, k):
            src = pos_ref[base + TOP_K * r + k]
            pltpu.make_async_copy(ys_hbm.at[pl.ds(src, 1)], buf.at[to_slot, k, pl.ds(r, 1)],
                                  sem.at[to_slot]).start()

        _for_each_row(tm, issue)

    @pl.when(i == 0)
    def _():
        gather(0, 0)

    @pl.when(i + 1 < n_tiles)
    def _():
        gather(i + 1, 1 - slot)

    for k in range(TOP_K):
        pltpu.make_async_copy(ys_hbm.at[pl.ds(0, tm)], buf.at[slot, k], sem.at[slot]).wait()
    meta = meta_ref[...]
    ff = meta[:, _META_GATE:_META_GATE + 1] * buf[slot, 0]
    for k in range(1, TOP_K):
        ff = ff + meta[:, _META_GATE + k:_META_GATE + k + 1] * buf[slot, k]
    y = _norm_ple(alpha * x_ref[...] + ff, p_ref[...], g_ref, b_ref, wg_ref, bg_ref, wp_ref)
    out_ref[...] = y
    outb_ref[...] = y.astype(_BF)


def _moe_post(x, ys, pos, meta, g, b, p, w_gate, b_gate, w_proj, alpha):
    n, d = x.shape
    pd = p.shape[1]
    tm = _pick(n, 256)
    tile = lambda width: pl.BlockSpec((tm, width), lambda i, pos: (i, 0))
    res = lambda shape: pl.BlockSpec(shape, lambda i, pos: (0,) * len(shape),
                                     pipeline_mode=pl.Buffered(1))
    return pl.pallas_call(
        functools.partial(_moe_post_kernel, alpha=alpha, tm=tm, n_tiles=n // tm),
        grid_spec=pltpu.PrefetchScalarGridSpec(
            num_scalar_prefetch=1,
            grid=(n // tm,),
            in_specs=[tile(d), tile(_LANES), pl.BlockSpec(memory_space=pl.ANY), res((1, d)),
                      res((1, d)), tile(pd), res((d, d)), res((1, d)), res((pd, d))],
            out_specs=[tile(d), tile(d)],
            scratch_shapes=[pltpu.VMEM((2, TOP_K, tm, d), _F32), pltpu.SemaphoreType.DMA((2,))],
        ),
        out_shape=[jax.ShapeDtypeStruct((n, d), _F32), jax.ShapeDtypeStruct((n, d), _BF)],
        compiler_params=_cparams("arbitrary"),
        name="moe_post",
    )(pos, x, meta, ys, g.reshape(1, d), b.reshape(1, d), p, w_gate, b_gate.reshape(1, d), w_proj)


_FUSED_MIXER_MIN_T = 256
_SPARSE_MOE_MIN_TOKENS = 4096
_GROUP_TILE_ROWS = 512


def _trunk(x, p, lru_conv, lru_h, cm_conv, w):
    bsz, t, d = x.shape
    depth = p.shape[0]
    n = bsz * t
    alpha = (2.0 * depth) ** 0.25
    n_mixers = 2
    x = x.reshape(n, d)
    xb = x
    new_lru_conv, new_lru_h, new_cm_conv = [], [], []
    for i in range(depth):
        j = i // n_mixers
        if i % n_mixers == 0:
            lw = w['rglru_w_out'].shape[1]
            gate, rec = _in_proj(xb, w['rglru_w_in'][j])
            core_args = (lru_conv[j], lru_h[j], w['rglru_conv_w'][j], w['rglru_conv_b'][j],
                         w['rglru_w_a'][j], w['rglru_b_a'][j], w['rglru_w_x'][j],
                         w['rglru_b_x'][j], w['rglru_lambda'][j])
            if t >= _FUSED_MIXER_MIN_T:
                x, xb, buf, hl = _rglru_layer(
                    rec.reshape(bsz, t, lw), gate.reshape(bsz, t, lw), x.reshape(bsz, t, d),
                    *core_args, w['rglru_w_out'][j], w['ln_mix_g'][i], w['ln_mix_b'][i], alpha)
                x, xb = x.reshape(n, d), xb.reshape(n, d)
            else:
                y, buf, hl = _rglru(rec.reshape(bsz, t, lw), gate.reshape(bsz, t, lw), *core_args)
                x, xb = _proj_ln(y.reshape(n, lw), w['rglru_w_out'][j], jnp.zeros((d,), _F32), x,
                                 w['ln_mix_g'][i], w['ln_mix_b'][i], alpha)
            new_lru_conv.append(buf)
            new_lru_h.append(hl)
        else:
            if t >= _FUSED_MIXER_MIN_T:
                x, xb, buf = _conv_layer(
                    xb.astype(_BF).reshape(bsz, t, d), x.reshape(bsz, t, d), cm_conv[j],
                    w['cm_w_pw1'][j], w['cm_b_pw1'][j], w['cm_dw_w'][j], w['cm_dw_b'][j],
                    w['cm_ln_g'][j], w['cm_ln_b'][j], w['cm_w_pw2'][j], w['cm_b_pw2'][j],
                    w['ln_mix_g'][i], w['ln_mix_b'][i], alpha)
                x, xb = x.reshape(n, d), xb.reshape(n, d)
            else:
                v = _pw1_glu(xb.astype(_BF), w['cm_w_pw1'][j], w['cm_b_pw1'][j])
                y, buf = _convmod(v.reshape(bsz, t, d), cm_conv[j], w['cm_dw_w'][j],
                                  w['cm_dw_b'][j], w['cm_ln_g'][j], w['cm_ln_b'][j])
                x, xb = _proj_ln(y.reshape(n, d), w['cm_w_pw2'][j], w['cm_b_pw2'][j], x,
                                 w['ln_mix_g'][i], w['ln_mix_b'][i], alpha)
            new_cm_conv.append(buf)
        k = i // 2
        post_args = (w['ln_ffn_g'][i], w['ln_ffn_b'][i], p[i].reshape(n, -1), w['ple_w_gate'][i],
                     w['ple_b_gate'][i], w['ple_w_proj'][i], alpha)
        if i % 2 == 0:
            ff = _ffn(xb, w['ffn_w_in'][k][None], w['ffn_w_out'][k][None])
            x, xb = _post(x, ff, *post_args)
        else:
            gates, meta, counts = _router(xb, w['moe_w_router'][k])
            ne = w['moe_w_router'].shape[-1]
            if n < _SPARSE_MOE_MIN_TOKENS:
                ff = _ffn(xb, w['moe_w_in'][k], w['moe_w_out'][k], gates, tf_pref=1408)
                x, xb = _post(x, ff, *post_args)
            else:
                tg = _GROUP_TILE_ROWS
                pos, ends, tile_expert, n_used, n_tiles = _route_plan(meta, counts, ne, tg)
                xs = _dispatch(x, pos, ends, n_tiles * tg, tg)
                ys = _grouped_ffn(xs, w['moe_w_in'][k], w['moe_w_out'][k], tile_expert, n_used,
                                  tg)
                x, xb = _moe_post(x, ys, pos, meta, *post_args)
    return (x.reshape(bsz, t, d), jnp.stack(new_lru_conv), jnp.stack(new_lru_h),
            jnp.stack(new_cm_conv))


_MATMUL_WEIGHTS = ('rglru_w_in', 'rglru_w_a', 'rglru_w_x', 'rglru_w_out', 'cm_w_pw1', 'cm_w_pw2',
                   'ffn_w_in', 'ffn_w_out', 'moe_w_in', 'moe_w_out', 'ple_w_proj', 'ple_w_gate')


def kernel(x_prompt, x_sample, p_prompt, p_sample, state_rglru_conv, state_rglru_h, state_conv_module, rglru_w_in, rglru_conv_w, rglru_conv_b, rglru_w_a, rglru_b_a, rglru_w_x, rglru_b_x, rglru_lambda, rglru_w_out, cm_w_pw1, cm_b_pw1, cm_dw_w, cm_dw_b, cm_ln_g, cm_ln_b, cm_w_pw2, cm_b_pw2, ffn_w_in, ffn_w_out, moe_w_router, moe_w_in, moe_w_out, ln_mix_g, ln_mix_b, ln_ffn_g, ln_ffn_b, ple_w_proj, ple_w_gate, ple_b_gate):
    w = dict(rglru_w_in=rglru_w_in, rglru_conv_w=rglru_conv_w, rglru_conv_b=rglru_conv_b,
             rglru_w_a=rglru_w_a, rglru_b_a=rglru_b_a, rglru_w_x=rglru_w_x, rglru_b_x=rglru_b_x,
             rglru_lambda=rglru_lambda, rglru_w_out=rglru_w_out, cm_w_pw1=cm_w_pw1,
             cm_b_pw1=cm_b_pw1, cm_dw_w=cm_dw_w, cm_dw_b=cm_dw_b, cm_ln_g=cm_ln_g,
             cm_ln_b=cm_ln_b, cm_w_pw2=cm_w_pw2, cm_b_pw2=cm_b_pw2, ffn_w_in=ffn_w_in,
             ffn_w_out=ffn_w_out, moe_w_router=moe_w_router, moe_w_in=moe_w_in,
             moe_w_out=moe_w_out, ln_mix_g=ln_mix_g, ln_mix_b=ln_mix_b, ln_ffn_g=ln_ffn_g,
             ln_ffn_b=ln_ffn_b, ple_w_proj=ple_w_proj, ple_w_gate=ple_w_gate,
             ple_b_gate=ple_b_gate)
    for name in _MATMUL_WEIGHTS:
        w[name] = w[name].astype(_BF)
    bp = x_prompt.shape[0]
    dt = x_prompt.dtype
    zero_conv = jnp.zeros((state_rglru_conv.shape[0], bp) + state_rglru_conv.shape[2:], dt)
    zero_h = jnp.zeros((state_rglru_h.shape[0], bp) + state_rglru_h.shape[2:], dt)
    zero_cm = jnp.zeros((state_conv_module.shape[0], bp) + state_conv_module.shape[2:], dt)
    y_p, conv_p, h_p, cm_p = _trunk(x_prompt, p_prompt, zero_conv, zero_h, zero_cm, w)
    y_s, conv_s, h_s, cm_s = _trunk(x_sample, p_sample, state_rglru_conv, state_rglru_h,
                                    state_conv_module, w)
    return (y_p, y_s, conv_p, h_p, cm_p, conv_s, h_s, cm_s)
```

```python
import functools

import jax
import jax.numpy as jnp
from jax import lax
from jax.experimental import pallas as pl
from jax.experimental.pallas import tpu as pltpu

_BF = jnp.bfloat16
_F32 = jnp.float32

LRU_C = 8.0
LN_EPS = 1e-5
TOP_K = 2

_V7X_VMEM_BYTES = 64 * 1024 * 1024
_VMEM_LIMIT = _V7X_VMEM_BYTES - 8 * 1024 * 1024
_VMEM_LIMIT_LARGE = _V7X_VMEM_BYTES - 4 * 1024 * 1024
_SUBLANES = 8
_LANES = 128
_V7X_MXU_DIM = 256
_ROW_BLOCK = 128


def _cparams(*sem):
    return pltpu.CompilerParams(dimension_semantics=sem, vmem_limit_bytes=_VMEM_LIMIT)


def _pick(n, pref, mult=_SUBLANES):
    best = None
    for c in range(mult, min(n, pref) + 1, mult):
        if n % c == 0:
            best = c
    assert best is not None, (n, pref, mult)
    return best


def _row_blocks(tm):
    rb = _pick(tm, _ROW_BLOCK)
    return [slice(r0, r0 + rb) for r0 in range(0, tm, rb)]


def _for_each_row(tm, issue):
    def group(g, c):
        r0 = pl.multiple_of(g * _SUBLANES, _SUBLANES)
        for j in range(_SUBLANES):
            for k in range(TOP_K):
                issue(r0 + j, k)
        return c

    lax.fori_loop(0, tm // _SUBLANES, group, 0)


def _resident(shape):
    nd = len(shape)
    return pl.BlockSpec(shape, lambda *_: (0,) * nd, pipeline_mode=pl.Buffered(1))


def _layer_norm(x, g, b):
    mu = jnp.mean(x, axis=-1, keepdims=True)
    xc = x - mu
    var = jnp.mean(xc * xc, axis=-1, keepdims=True)
    return xc * lax.rsqrt(var + LN_EPS) * g + b


def _sigmoid(x):
    return 1.0 / (1.0 + jnp.exp(-x))


def _gelu_tanh(x):
    c = 0.7978845608028654
    return 0.5 * x * (1.0 + jnp.tanh(c * (x + 0.044715 * (x * x * x))))


def _softplus(x):
    return jnp.maximum(x, 0.0) + jnp.log1p(jnp.exp(-jnp.abs(x)))


def _in_proj_kernel(x_ref, w_ref, gate_ref, rec_ref):
    lw = gate_ref.shape[-1]
    for rows in _row_blocks(x_ref.shape[0]):
        xb = x_ref[rows, :].astype(_BF)
        gate_ref[rows, :] = _gelu_tanh(jnp.dot(xb, w_ref[:, :lw], preferred_element_type=_F32))
        rec_ref[rows, :] = jnp.dot(xb, w_ref[:, lw:], preferred_element_type=_F32)


def _in_proj(x, w):
    n, d = x.shape
    lw = w.shape[1] // 2
    tm = _pick(n, 512)
    tile = lambda width: pl.BlockSpec((tm, width), lambda i: (i, 0))
    return pl.pallas_call(
        _in_proj_kernel,
        grid=(n // tm,),
        in_specs=[tile(d), _resident((d, 2 * lw))],
        out_specs=[tile(lw), tile(lw)],
        out_shape=[jax.ShapeDtypeStruct((n, lw), _F32)] * 2,
        compiler_params=_cparams("parallel"),
        name="in_proj",
    )(x, w)


def _rglru_kernel(rec_ref, gate_ref, buf0_ref, h0_ref, cw_ref, cb_ref, wa_ref, ba_ref, wx_ref,
                  bx_ref, lam_ref, y_ref, nbuf_ref, hlast_ref, xp_scr, a_scr, b_scr, carry_scr,
                  *, tc, heads, kw):
    t = pl.program_id(1)
    lw = rec_ref.shape[-1]
    blk = lw // heads
    hist = _SUBLANES
    base = hist - (kw - 1)

    @pl.when(t == 0)
    def _():
        xp_scr[0:hist, :] = buf0_ref[0]
        carry_scr[...] = jnp.broadcast_to(h0_ref[0], (_SUBLANES, lw))

    xp_scr[hist:hist + tc, :] = rec_ref[0]
    decay = -LRU_C * _softplus(-lam_ref[...])
    for h in range(heads):
        sl = slice(h * blk, (h + 1) * blk)
        whole = xp_scr[:, sl]
        xc = jnp.broadcast_to(cb_ref[:, sl], (tc, blk))
        for k in range(kw):
            j = (base + k) % _SUBLANES
            q0 = (base + k) - j
            win = whole if j == 0 else pltpu.roll(whole, hist + tc - j, 0)
            xc = xc + cw_ref[k:k + 1, sl] * win[q0:q0 + tc]
        xcb = xc.astype(_BF)
        r = _sigmoid(jnp.dot(xcb, wa_ref[h], preferred_element_type=_F32) + ba_ref[:, sl])
        ig = _sigmoid(jnp.dot(xcb, wx_ref[h], preferred_element_type=_F32) + bx_ref[:, sl])
        log_a = r * decay[:, sl]
        a = jnp.exp(log_a)
        a_scr[:, sl] = a
        b_scr[:, sl] = jnp.sqrt(-jnp.tanh(log_a) * (a * a + 1.0)) * (ig * xc)

    row = lax.broadcasted_iota(jnp.int32, (_SUBLANES, lw), 0)

    def body(g, carry):
        r0 = pl.multiple_of(g * _SUBLANES, _SUBLANES)
        a = a_scr[pl.ds(r0, _SUBLANES), :]
        b = b_scr[pl.ds(r0, _SUBLANES), :]
        for s in (1, 2, 4):
            m = row >= s
            a_sh = jnp.where(m, pltpu.roll(a, s, 0), 1.0)
            b_sh = jnp.where(m, pltpu.roll(b, s, 0), 0.0)
            b = a * b_sh + b
            a = a * a_sh
        hrows = a * carry + b
        b_scr[pl.ds(r0, _SUBLANES), :] = hrows
        return jnp.broadcast_to(hrows[_SUBLANES - 1:_SUBLANES, :], (_SUBLANES, lw))

    carry = lax.fori_loop(0, tc // _SUBLANES, body, carry_scr[...])
    carry_scr[...] = carry
    y_ref[0] = (b_scr[...] * gate_ref[0]).astype(y_ref.dtype)
    hlast_ref[0] = carry
    tail = xp_scr[tc:tc + hist, :]
    nbuf_ref[0] = tail
    xp_scr[0:hist, :] = tail


def _rglru_layer_kernel(rec_ref, gate_ref, buf0_ref, h0_ref, cw_ref, cb_ref, wa_ref, ba_ref,
                        wx_ref, bx_ref, lam_ref, x_ref, wo_ref, g_ref, b_ref, out_ref, outb_ref,
                        nbuf_ref, hlast_ref, xp_scr, a_scr, b_scr, carry_scr, y_scr, mix_scr,
                        *, tc, heads, kw, n_chunks, n_live, alpha):
    s = pl.program_id(0)
    t = s % n_chunks
    live = s < n_live
    lw = rec_ref.shape[-1]
    blk = lw // heads
    hist = _SUBLANES
    base = hist - (kw - 1)

    @pl.when(s == 0)
    def _():
        y_scr[...] = jnp.zeros_like(y_scr)

    @pl.when((t == 0) & live)
    def _():
        xp_scr[0:hist, :] = buf0_ref[0]
        carry_scr[...] = jnp.broadcast_to(h0_ref[0], (_SUBLANES, lw))

    xp_scr[hist:hist + tc, :] = rec_ref[0]
    d = out_ref.shape[-1]
    dblk = d // heads
    y_prev = y_scr[...]
    decay = -LRU_C * _softplus(-lam_ref[...])
    for h in range(heads):
        cols = slice(h * dblk, (h + 1) * dblk)
        mix_scr[:, cols] = jnp.dot(y_prev, wo_ref[:, cols], preferred_element_type=_F32)
        sl = slice(h * blk, (h + 1) * blk)
        whole = xp_scr[:, sl]
        xc = jnp.broadcast_to(cb_ref[:, sl], (tc, blk))
        for k in range(kw):
            j = (base + k) % _SUBLANES
            q0 = (base + k) - j
            win = whole if j == 0 else pltpu.roll(whole, hist + tc - j, 0)
            xc = xc + cw_ref[k:k + 1, sl] * win[q0:q0 + tc]
        xcb = xc.astype(_BF)
        r = _sigmoid(jnp.dot(xcb, wa_ref[h], preferred_element_type=_F32) + ba_ref[:, sl])
        ig = _sigmoid(jnp.dot(xcb, wx_ref[h], preferred_element_type=_F32) + bx_ref[:, sl])
        log_a = r * decay[:, sl]
        a = jnp.exp(log_a)
        a_scr[:, sl] = a
        b_scr[:, sl] = jnp.sqrt(-jnp.tanh(log_a) * (a * a + 1.0)) * (ig * xc)
    xo = _layer_norm(alpha * x_ref[0] + mix_scr[...], g_ref[...], b_ref[...])
    out_ref[0] = xo
    outb_ref[0] = xo.astype(_BF)

    row = lax.broadcasted_iota(jnp.int32, (_SUBLANES, lw), 0)

    def body(g, carry):
        r0 = pl.multiple_of(g * _SUBLANES, _SUBLANES)
        a = a_scr[pl.ds(r0, _SUBLANES), :]
        b = b_scr[pl.ds(r0, _SUBLANES), :]
        for sh in (1, 2, 4):
            m = row >= sh
            a_sh = jnp.where(m, pltpu.roll(a, sh, 0), 1.0)
            b_sh = jnp.where(m, pltpu.roll(b, sh, 0), 0.0)
            b = a * b_sh + b
            a = a * a_sh
        hrows = a * carry + b
        b_scr[pl.ds(r0, _SUBLANES), :] = hrows
        return jnp.broadcast_to(hrows[_SUBLANES - 1:_SUBLANES, :], (_SUBLANES, lw))

    carry = lax.fori_loop(0, tc // _SUBLANES, body, carry_scr[...])
    y_scr[...] = (b_scr[...] * gate_ref[0]).astype(_BF)

    @pl.when(live)
    def _():
        carry_scr[...] = carry
        hlast_ref[0] = carry
        tail = xp_scr[tc:tc + hist, :]
        nbuf_ref[0] = tail
        xp_scr[0:hist, :] = tail


def _rglru_layer(rec, gate, x, buf0, h0, conv_w, conv_b, w_a, b_a, w_x, b_x, lam, w_out, ln_g,
                 ln_b, alpha):
    bsz, t, lw = rec.shape
    d = x.shape[-1]
    heads, blk, _ = w_a.shape
    kw = conv_w.shape[0]
    assert kw - 1 <= _SUBLANES <= t and t % _SUBLANES == 0
    tc = _pick(t, 256)
    n_chunks = t // tc
    n_live = bsz * n_chunks
    buf_pad = jnp.pad(buf0, ((0, 0), (_SUBLANES - (kw - 1), 0), (0, 0)))
    row = lambda v: v.reshape(1, -1)

    def cur(s):
        c = jnp.minimum(s, n_live - 1)
        return c // n_chunks, c % n_chunks

    def prev(s):
        c = jnp.maximum(s - 1, 0)
        return c // n_chunks, c % n_chunks

    cur_seq = pl.BlockSpec((1, tc, lw), lambda s: (*cur(s), 0))
    prev_seq = pl.BlockSpec((1, tc, d), lambda s: (*prev(s), 0))
    per_b = pl.BlockSpec((1, _SUBLANES, lw), lambda s: (cur(s)[0], 0, 0))
    out, outb, nbuf, hlast = pl.pallas_call(
        functools.partial(_rglru_layer_kernel, tc=tc, heads=heads, kw=kw, n_chunks=n_chunks,
                          n_live=n_live, alpha=alpha),
        grid=(n_live + 1,),
        in_specs=[
            cur_seq, cur_seq, per_b,
            pl.BlockSpec((1, 1, lw), lambda s: (cur(s)[0], 0, 0)),
            _resident((kw, lw)), _resident((1, lw)),
            _resident((heads, blk, blk)), _resident((1, lw)),
            _resident((heads, blk, blk)), _resident((1, lw)),
            _resident((1, lw)),
            prev_seq, _resident((lw, d)), _resident((1, d)), _resident((1, d)),
        ],
        out_specs=[prev_seq, prev_seq, per_b, per_b],
        out_shape=[
            jax.ShapeDtypeStruct((bsz, t, d), _F32),
            jax.ShapeDtypeStruct((bsz, t, d), _BF),
            jax.ShapeDtypeStruct((bsz, _SUBLANES, lw), _F32),
            jax.ShapeDtypeStruct((bsz, _SUBLANES, lw), _F32),
        ],
        scratch_shapes=[
            pltpu.VMEM((_SUBLANES + tc, lw), _F32),
            pltpu.VMEM((tc, lw), _F32),
            pltpu.VMEM((tc, lw), _F32),
            pltpu.VMEM((_SUBLANES, lw), _F32),
            pltpu.VMEM((tc, lw), _BF),
            pltpu.VMEM((tc, d), _F32),
        ],
        compiler_params=_cparams("arbitrary"),
        name="rglru_layer",
    )(rec, gate, buf_pad, h0.reshape(bsz, 1, lw), conv_w, row(conv_b), w_a, row(b_a), w_x,
      row(b_x), row(lam), x, w_out, row(ln_g), row(ln_b))
    return out, outb, nbuf[:, _SUBLANES - (kw - 1):], hlast[:, 0]


def _rglru(rec, gate, buf0, h0, conv_w, conv_b, w_a, b_a, w_x, b_x, lam):
    bsz, t, lw = rec.shape
    heads, blk, _ = w_a.shape
    kw = conv_w.shape[0]
    assert kw - 1 <= _SUBLANES <= t and t % _SUBLANES == 0
    tc = _pick(t, 256)
    buf_pad = jnp.pad(buf0, ((0, 0), (_SUBLANES - (kw - 1), 0), (0, 0)))
    row = lambda v: v.reshape(1, lw)
    seq = pl.BlockSpec((1, tc, lw), lambda b, i: (b, i, 0))
    per_b = pl.BlockSpec((1, _SUBLANES, lw), lambda b, i: (b, 0, 0))
    y, nbuf, hlast = pl.pallas_call(
        functools.partial(_rglru_kernel, tc=tc, heads=heads, kw=kw),
        grid=(bsz, t // tc),
        in_specs=[
            seq, seq, per_b,
            pl.BlockSpec((1, 1, lw), lambda b, i: (b, 0, 0)),
            _resident((kw, lw)), _resident((1, lw)),
            _resident((heads, blk, blk)), _resident((1, lw)),
            _resident((heads, blk, blk)), _resident((1, lw)),
            _resident((1, lw)),
        ],
        out_specs=[seq, per_b, per_b],
        out_shape=[
            jax.ShapeDtypeStruct((bsz, t, lw), _BF),
            jax.ShapeDtypeStruct((bsz, _SUBLANES, lw), _F32),
            jax.ShapeDtypeStruct((bsz, _SUBLANES, lw), _F32),
        ],
        scratch_shapes=[
            pltpu.VMEM((_SUBLANES + tc, lw), _F32),
            pltpu.VMEM((tc, lw), _F32),
            pltpu.VMEM((tc, lw), _F32),
            pltpu.VMEM((_SUBLANES, lw), _F32),
        ],
        compiler_params=_cparams("parallel", "arbitrary"),
        name="rglru",
    )(rec, gate, buf_pad, h0.reshape(bsz, 1, lw), conv_w, row(conv_b), w_a, row(b_a), w_x,
      row(b_x), row(lam))
    return y, nbuf[:, _SUBLANES - (kw - 1):], hlast[:, 0]


def _proj_ln_kernel(a_ref, w_ref, bias_ref, x_ref, g_ref, b_ref, out_ref, outb_ref, *, alpha):
    for rows in _row_blocks(a_ref.shape[0]):
        mix = jnp.dot(a_ref[rows, :], w_ref[...], preferred_element_type=_F32) + bias_ref[...]
        y = _layer_norm(alpha * x_ref[rows, :] + mix, g_ref[...], b_ref[...])
        out_ref[rows, :] = y
        outb_ref[rows, :] = y.astype(_BF)


def _proj_ln(a, w, bias, x, g, b, alpha):
    n, k = a.shape
    d = w.shape[1]
    tm = _pick(n, 512)
    tile = lambda width: pl.BlockSpec((tm, width), lambda i: (i, 0))
    return pl.pallas_call(
        functools.partial(_proj_ln_kernel, alpha=alpha),
        grid=(n // tm,),
        in_specs=[tile(k), _resident((k, d)), _resident((1, d)), tile(d), _resident((1, d)),
                  _resident((1, d))],
        out_specs=[tile(d), tile(d)],
        out_shape=[jax.ShapeDtypeStruct((n, d), _F32), jax.ShapeDtypeStruct((n, d), _BF)],
        compiler_params=_cparams("parallel"),
        name="proj_ln",
    )(a, w, bias.reshape(1, d), x, g.reshape(1, d), b.reshape(1, d))


def _ffn_kernel(x_ref, wg_ref, wu_ref, wo_ref, *rest, gated):
    if gated:
        gates_ref, out_ref = rest
    else:
        (out_ref,) = rest
    e = pl.program_id(1)
    f = pl.program_id(2)

    @pl.when((e == 0) & (f == 0))
    def _():
        out_ref[...] = jnp.zeros_like(out_ref)

    xb = x_ref[...]
    g = jnp.dot(xb, wg_ref[0], preferred_element_type=_F32)
    u = jnp.dot(xb, wu_ref[0], preferred_element_type=_F32)
    h = (g * _sigmoid(g)) * u
    if gated:
        lane = lax.broadcasted_iota(jnp.int32, gates_ref.shape, 1)
        h = h * jnp.sum(jnp.where(lane == e, gates_ref[...], 0.0), axis=1, keepdims=True)
    out_ref[...] += jnp.dot(h.astype(_BF), wo_ref[0], preferred_element_type=_F32)


def _ffn(xb, w_in, w_out, gates=None, tf_pref=512):
    n, d = xb.shape
    ne, f, _ = w_out.shape
    tm = _pick(n, 1024)
    tf = _pick(f, tf_pref, _LANES)
    nf = f // tf
    in_specs = [
        pl.BlockSpec((tm, d), lambda i, e, j: (i, 0)),
        pl.BlockSpec((1, d, tf), lambda i, e, j: (e, 0, j)),
        pl.BlockSpec((1, d, tf), lambda i, e, j: (e, 0, j + nf)),
        pl.BlockSpec((1, tf, d), lambda i, e, j: (e, j, 0)),
    ]
    args = [xb, w_in, w_in, w_out]
    if gates is not None:
        in_specs.append(pl.BlockSpec((tm, _LANES), lambda i, e, j: (i, 0)))
        args.append(gates)
    return pl.pallas_call(
        functools.partial(_ffn_kernel, gated=gates is not None),
        grid=(n // tm, ne, nf),
        in_specs=in_specs,
        out_specs=pl.BlockSpec((tm, d), lambda i, e, j: (i, 0)),
        out_shape=jax.ShapeDtypeStruct((n, d), _F32),
        compiler_params=_cparams("parallel", "arbitrary", "arbitrary"),
        name="ffn",
    )(*args)


def _norm_ple(pre, p, g_ref, b_ref, wg_ref, bg_ref, wp_ref):
    xn = _layer_norm(pre, g_ref[...], b_ref[...])
    gate = _sigmoid(jnp.dot(xn.astype(_BF), wg_ref[...], preferred_element_type=_F32)
                    + bg_ref[...])
    proj = jnp.dot(p.astype(_BF), wp_ref[...], preferred_element_type=_F32)
    return xn + gate * proj


def _post_kernel(x_ref, ff_ref, g_ref, b_ref, p_ref, wg_ref, bg_ref, wp_ref, out_ref, outb_ref,
                 *, alpha):
    y = _norm_ple(alpha * x_ref[...] + ff_ref[...], p_ref[...], g_ref, b_ref, wg_ref, bg_ref,
                  wp_ref)
    out_ref[...] = y
    outb_ref[...] = y.astype(_BF)


def _post(x, ff, g, b, p, w_gate, b_gate, w_proj, alpha):
    n, d = x.shape
    pd = p.shape[1]
    tm = _pick(n, 512)
    tile = lambda width: pl.BlockSpec((tm, width), lambda i: (i, 0))
    return pl.pallas_call(
        functools.partial(_post_kernel, alpha=alpha),
        grid=(n // tm,),
        in_specs=[tile(d), tile(d), _resident((1, d)), _resident((1, d)), tile(pd),
                  _resident((d, d)), _resident((1, d)), _resident((pd, d))],
        out_specs=[tile(d), tile(d)],
        out_shape=[jax.ShapeDtypeStruct((n, d), _F32), jax.ShapeDtypeStruct((n, d), _BF)],
        compiler_params=_cparams("parallel"),
        name="post",
    )(x, ff, g.reshape(1, d), b.reshape(1, d), p, w_gate, b_gate.reshape(1, d), w_proj)


def _pw1_glu_kernel(x_ref, wv_ref, wg_ref, bv_ref, bg_ref, v_ref):
    for rows in _row_blocks(x_ref.shape[0]):
        xb = x_ref[rows, :]
        val = jnp.dot(xb, wv_ref[...], preferred_element_type=_F32) + bv_ref[...]
        gate = jnp.dot(xb, wg_ref[...], preferred_element_type=_F32) + bg_ref[...]
        v_ref[rows, :] = val * _sigmoid(gate)


def _pw1_glu(xb, w, bias):
    n, d = xb.shape
    dv = w.shape[1] // 2
    tm = _pick(n, 1024)
    tn = _pick(dv, 512, _LANES)
    nj = dv // tn
    bias = bias.reshape(1, 2 * dv)
    return pl.pallas_call(
        _pw1_glu_kernel,
        grid=(n // tm, nj),
        in_specs=[
            pl.BlockSpec((tm, d), lambda i, j: (i, 0)),
            pl.BlockSpec((d, tn), lambda i, j: (0, j)),
            pl.BlockSpec((d, tn), lambda i, j: (0, j + nj)),
            pl.BlockSpec((1, tn), lambda i, j: (0, j)),
            pl.BlockSpec((1, tn), lambda i, j: (0, j + nj)),
        ],
        out_specs=pl.BlockSpec((tm, tn), lambda i, j: (i, j)),
        out_shape=jax.ShapeDtypeStruct((n, dv), _F32),
        compiler_params=_cparams("parallel", "arbitrary"),
        name="pw1_glu",
    )(xb, w, w, bias, bias)


def _convmod_kernel(v_ref, buf0_ref, w_ref, b_ref, g_ref, beta_ref, y_ref, nbuf_ref, vp_scr,
                    c_scr, *, tc, ksz, hist):
    t = pl.program_id(1)
    d = v_ref.shape[-1]
    base = hist - (ksz - 1)

    @pl.when(t == 0)
    def _():
        vp_scr[0:hist, :] = buf0_ref[0]

    vp_scr[hist:hist + tc, :] = v_ref[0]
    _depthwise_conv(vp_scr, 0, tc, hist, w_ref, b_ref, ksz, c_scr)
    z = _layer_norm(c_scr[...], g_ref[...], beta_ref[...])
    y_ref[0] = (z * _sigmoid(z)).astype(y_ref.dtype)
    tail = vp_scr[tc:tc + hist, :]
    nbuf_ref[0] = tail
    vp_scr[0:hist, :] = tail


def _depthwise_conv(vp_scr, r0, nrows, hist, w_ref, b_ref, ksz, c_scr):
    d = vp_scr.shape[-1]
    base = hist - (ksz - 1)
    rows = hist + nrows
    for c in range(d // _LANES):
        sl = slice(c * _LANES, (c + 1) * _LANES)
        acc = jnp.broadcast_to(b_ref[:, sl], (nrows, _LANES))
        whole = vp_scr[r0:r0 + rows, sl]
        for j in range(_SUBLANES):
            taps = [k for k in range(ksz) if (base + k) % _SUBLANES == j]
            if not taps:
                continue
            win = whole if j == 0 else pltpu.roll(whole, rows - j, 0)
            for k in taps:
                q0 = ((base + k) // _SUBLANES) * _SUBLANES
                acc = acc + w_ref[k:k + 1, sl] * win[q0:q0 + nrows]
        c_scr[r0:r0 + nrows, sl] = acc


def _conv_layer_kernel(xb_ref, x_ref, buf0_ref, w1_ref, b1_ref, dw_ref, db_ref, cg_ref, cb_ref,
                       w2_ref, b2_ref, g_ref, b_ref, out_ref, outb_ref, nbuf_ref, vp_scr, c_scr,
                       *, tc, sub, ksz, hist, alpha):
    d = x_ref.shape[-1]

    @pl.when(pl.program_id(1) == 0)
    def _():
        vp_scr[0:hist, :] = buf0_ref[0]

    for r0 in range(0, tc, sub):
        rows = slice(r0, r0 + sub)
        xb = xb_ref[0, rows, :]
        val = jnp.dot(xb, w1_ref[:, :d], preferred_element_type=_F32) + b1_ref[:, :d]
        gate = jnp.dot(xb, w1_ref[:, d:], preferred_element_type=_F32) + b1_ref[:, d:]
        vp_scr[hist + r0:hist + r0 + sub, :] = val * _sigmoid(gate)
        _depthwise_conv(vp_scr, r0, sub, hist, dw_ref, db_ref, ksz, c_scr)
        z = _layer_norm(c_scr[rows, :], cg_ref[...], cb_ref[...])
        y = (z * _sigmoid(z)).astype(_BF)
        mix = jnp.dot(y, w2_ref[...], preferred_element_type=_F32) + b2_ref[...]
        xo = _layer_norm(alpha * x_ref[0, rows, :] + mix, g_ref[...], b_ref[...])
        out_ref[0, rows, :] = xo
        outb_ref[0, rows, :] = xo.astype(_BF)
    tail = vp_scr[tc:tc + hist, :]
    nbuf_ref[0] = tail
    vp_scr[0:hist, :] = tail


def _conv_layer(xb, x, buf0, w_pw1, b_pw1, dw_w, dw_b, cm_g, cm_b, w_pw2, b_pw2, ln_g, ln_b,
                alpha):
    bsz, t, d = x.shape
    ksz = dw_w.shape[0]
    hist = -(-(ksz - 1) // _SUBLANES) * _SUBLANES
    assert t % _SUBLANES == 0 and d % _LANES == 0
    tc = _pick(t, 256)
    sub = _pick(tc, 128)
    buf_pad = jnp.pad(buf0, ((0, 0), (hist - (ksz - 1), 0), (0, 0)))
    row = lambda a: a.reshape(1, -1)
    seq = pl.BlockSpec((1, tc, d), lambda b, i: (b, i, 0))
    per_b = pl.BlockSpec((1, hist, d), lambda b, i: (b, 0, 0))
    out, outb, nbuf = pl.pallas_call(
        functools.partial(_conv_layer_kernel, tc=tc, sub=sub, ksz=ksz, hist=hist, alpha=alpha),
        grid=(bsz, t // tc),
        in_specs=[seq, seq, per_b, _resident((d, 2 * d)), _resident((1, 2 * d)),
                  _resident((ksz, d)), _resident((1, d)), _resident((1, d)), _resident((1, d)),
                  _resident((d, d)), _resident((1, d)), _resident((1, d)), _resident((1, d))],
        out_specs=[seq, seq, per_b],
        out_shape=[jax.ShapeDtypeStruct((bsz, t, d), _F32), jax.ShapeDtypeStruct((bsz, t, d), _BF),
                   jax.ShapeDtypeStruct((bsz, hist, d), _F32)],
        scratch_shapes=[pltpu.VMEM((hist + tc, d), _F32), pltpu.VMEM((tc, d), _F32)],
        compiler_params=_cparams("parallel", "arbitrary"),
        name="conv_layer",
    )(xb, x, buf_pad, w_pw1, row(b_pw1), dw_w, row(dw_b), row(cm_g), row(cm_b), w_pw2,
      row(b_pw2), row(ln_g), row(ln_b))
    return out, outb, nbuf[:, hist - (ksz - 1):]


def _convmod(v, buf0, dw_w, dw_b, ln_g, ln_b):
    bsz, t, d = v.shape
    ksz = dw_w.shape[0]
    hist = -(-(ksz - 1) // _SUBLANES) * _SUBLANES
    assert t % _SUBLANES == 0 and d % _LANES == 0
    tc = _pick(t, 128)
    buf_pad = jnp.pad(buf0, ((0, 0), (hist - (ksz - 1), 0), (0, 0)))
    row = lambda a: a.reshape(1, d)
    seq = pl.BlockSpec((1, tc, d), lambda b, i: (b, i, 0))
    per_b = pl.BlockSpec((1, hist, d), lambda b, i: (b, 0, 0))
    y, nbuf = pl.pallas_call(
        functools.partial(_convmod_kernel, tc=tc, ksz=ksz, hist=hist),
        grid=(bsz, t // tc),
        in_specs=[seq, per_b, _resident((ksz, d)), _resident((1, d)), _resident((1, d)),
                  _resident((1, d))],
        out_specs=[seq, per_b],
        out_shape=[jax.ShapeDtypeStruct((bsz, t, d), _BF),
                   jax.ShapeDtypeStruct((bsz, hist, d), _F32)],
        scratch_shapes=[pltpu.VMEM((hist + tc, d), _F32), pltpu.VMEM((tc, d), _F32)],
        compiler_params=_cparams("parallel", "arbitrary"),
        name="convmod",
    )(v, buf_pad, dw_w, row(dw_b), row(ln_g), row(ln_b))
    return y, nbuf[:, hist - (ksz - 1):]


_META_EXPERT, _META_GATE, _META_RANK = 0, TOP_K, 2 * TOP_K


def _router_kernel(xb_ref, w_ref, gates_ref, meta_ref, counts_ref, run_scr, *, n_experts):
    @pl.when(pl.program_id(0) == 0)
    def _():
        run_scr[...] = jnp.zeros_like(run_scr)

    logits = jnp.dot(xb_ref[...], w_ref[...], preferred_element_type=_F32)
    tm = logits.shape[0]
    lane = lax.broadcasted_iota(jnp.int32, logits.shape, 1)
    valid = lane < n_experts
    logits = jnp.where(valid, logits, -1e30)
    ex = jnp.where(valid, jnp.exp(logits - jnp.max(logits, axis=1, keepdims=True)), 0.0)
    probs = ex / jnp.sum(ex, axis=1, keepdims=True)
    rest = jnp.where(valid, probs, -1.0)
    total = jnp.zeros_like(probs[:, :1])
    picks = []
    for _ in range(TOP_K):
        top = jnp.max(rest, axis=1, keepdims=True)
        idx = jnp.min(jnp.where(rest == top, lane, _LANES), axis=1, keepdims=True)
        hit = lane == idx
        picks.append((top, idx, hit))
        total = total + top
        rest = jnp.where(hit, -1.0, rest)
    gates = jnp.zeros_like(probs)
    chosen = jnp.zeros_like(probs)
    for top, _, hit in picks:
        gates = jnp.where(hit, top / total, gates)
        chosen = jnp.where(hit, 1.0, chosen)
    gates_ref[...] = gates
    r_i = lax.broadcasted_iota(jnp.int32, (tm, tm), 0)
    c_i = lax.broadcasted_iota(jnp.int32, (tm, tm), 1)
    tri = jnp.where(r_i > c_i, 1.0, 0.0).astype(_BF)
    before = jnp.dot(tri, chosen.astype(_BF), preferred_element_type=_F32) + run_scr[...]
    meta = jnp.zeros_like(probs)
    for k, (top, idx, hit) in enumerate(picks):
        rank = jnp.sum(jnp.where(hit, before, 0.0), axis=1, keepdims=True)
        meta = jnp.where(lane == _META_EXPERT + k, idx.astype(_F32), meta)
        meta = jnp.where(lane == _META_GATE + k, top / total, meta)
        meta = jnp.where(lane == _META_RANK + k, rank, meta)
    meta_ref[...] = meta
    run_scr[...] += jnp.sum(chosen, axis=0, keepdims=True)
    counts_ref[...] = run_scr[...]


def _router(xb, w_router):
    n, d = xb.shape
    ne = w_router.shape[1]
    assert ne <= _LANES and 3 * TOP_K <= _LANES
    wpad = jnp.pad(w_router, ((0, 0), (0, _LANES - ne))).astype(_BF)
    tm = _pick(n, 512)
    tile = pl.BlockSpec((tm, _LANES), lambda i: (i, 0))
    return pl.pallas_call(
        functools.partial(_router_kernel, n_experts=ne),
        grid=(n // tm,),
        in_specs=[pl.BlockSpec((tm, d), lambda i: (i, 0)), _resident((d, _LANES))],
        out_specs=[tile, tile, pl.BlockSpec((1, _LANES), lambda i: (0, 0))],
        out_shape=[jax.ShapeDtypeStruct((n, _LANES), _F32), jax.ShapeDtypeStruct((n, _LANES), _F32),
                   jax.ShapeDtypeStruct((1, _LANES), _F32)],
        scratch_shapes=[pltpu.VMEM((1, _LANES), _F32)],
        compiler_params=_cparams("arbitrary"),
        name="router",
    )(xb, wpad)


def _route_plan(meta, counts_row, n_experts, tg):
    n = meta.shape[0]
    expert = meta[:, _META_EXPERT:_META_EXPERT + TOP_K].astype(jnp.int32)
    rank = meta[:, _META_RANK:_META_RANK + TOP_K].astype(jnp.int32)
    counts = counts_row[0, :n_experts].astype(jnp.int32)
    padded = (counts + tg - 1) // tg * tg
    ends = jnp.cumsum(padded)
    starts = ends - padded
    pos = (jnp.take(starts, expert) + rank).reshape(-1)
    n_tiles = -(-(TOP_K * n + n_experts * (tg - 1)) // tg)
    tile_start = jnp.arange(n_tiles, dtype=jnp.int32) * tg
    tile_expert = jnp.sum((tile_start[:, None] >= ends[None, :]).astype(jnp.int32), axis=1)
    tile_expert = jnp.minimum(tile_expert, n_experts - 1)
    n_used = (ends[-1:] // tg).astype(jnp.int32)
    return pos, ends, tile_expert, n_used, n_tiles


def _dispatch_kernel(pos_ref, ends_ref, x_ref, out_hbm, sem, *, tm, tg, n_experts):
    base = pl.program_id(0) * (TOP_K * tm)

    def fill_tile(row0):
        for c in range(tg // tm):
            rows = pl.ds(pl.multiple_of(row0 + c * tm, tm), tm)
            fill = pltpu.make_async_copy(x_ref, out_hbm.at[rows], sem)
            fill.start()
            fill.wait()

    @pl.when(pl.program_id(0) == 0)
    def _():
        for e in range(n_experts):
            prev_end = ends_ref[e - 1] if e else 0

            @pl.when(ends_ref[e] > prev_end)
            def _():
                fill_tile(ends_ref[e] - tg)

        for u in range(1, min(n_experts, out_hbm.shape[0] // tg) + 1):
            row0 = out_hbm.shape[0] - u * tg

            @pl.when(row0 >= ends_ref[n_experts - 1])
            def _():
                fill_tile(row0)

    def issue(r, k):
        dst = pos_ref[base + TOP_K * r + k]
        pltpu.make_async_copy(x_ref.at[pl.ds(r, 1)], out_hbm.at[pl.ds(dst, 1)], sem).start()

    _for_each_row(tm, issue)
    for k in range(TOP_K):
        pltpu.make_async_copy(x_ref, out_hbm.at[pl.ds(0, tm)], sem).wait()


def _dispatch(x, pos, ends, n_rows, tg):
    n, d = x.shape
    tm = _pick(n, 256)
    assert tg % tm == 0
    return pl.pallas_call(
        functools.partial(_dispatch_kernel, tm=tm, tg=tg, n_experts=ends.shape[0]),
        grid_spec=pltpu.PrefetchScalarGridSpec(
            num_scalar_prefetch=2,
            grid=(n // tm,),
            in_specs=[pl.BlockSpec((tm, d), lambda i, pos, ends: (i, 0))],
            out_specs=pl.BlockSpec(memory_space=pl.ANY),
            scratch_shapes=[pltpu.SemaphoreType.DMA(())],
        ),
        out_shape=jax.ShapeDtypeStruct((n_rows, d), x.dtype),
        compiler_params=_cparams("arbitrary"),
        name="dispatch",
    )(pos, ends, x)


def _grouped_ffn_kernel(te_ref, nu_ref, x_ref, wi_ref, wo_ref, out_ref, *, chunk):
    del te_ref
    f = wo_ref.shape[1]

    @pl.when(pl.program_id(0) < nu_ref[0])
    def _():
        xb = x_ref[...].astype(_BF)
        for c0 in range(0, f, chunk):
            c1 = min(c0 + chunk, f)
            g = jnp.dot(xb, wi_ref[0, :, c0:c1], preferred_element_type=_F32)
            u = jnp.dot(xb, wi_ref[0, :, f + c0:f + c1], preferred_element_type=_F32)
            h = ((g * _sigmoid(g)) * u).astype(_BF)
            part = jnp.dot(h, wo_ref[0, c0:c1, :], preferred_element_type=_F32)
            if c0 == 0:
                out_ref[...] = part
            else:
                out_ref[...] += part

    @pl.when(pl.program_id(0) >= nu_ref[0])
    def _():
        out_ref[...] = jnp.zeros_like(out_ref)


def _grouped_ffn(xs, w_in, w_out, tile_expert, n_used, tg):
    r, d = xs.shape
    f = w_out.shape[1]
    expert_block = lambda shape: pl.BlockSpec(shape, lambda i, te, nu: (te[i], 0, 0),
                                              pipeline_mode=pl.Buffered(1))
    return pl.pallas_call(
        functools.partial(_grouped_ffn_kernel, chunk=2 * _V7X_MXU_DIM),
        grid_spec=pltpu.PrefetchScalarGridSpec(
            num_scalar_prefetch=2,
            grid=(r // tg,),
            in_specs=[
                pl.BlockSpec((tg, d), lambda i, te, nu: (jnp.minimum(i, nu[0] - 1), 0)),
                expert_block((1, d, 2 * f)),
                expert_block((1, f, d)),
            ],
            out_specs=pl.BlockSpec((tg, d), lambda i, te, nu: (i, 0)),
        ),
        out_shape=jax.ShapeDtypeStruct((r, d), _F32),
        compiler_params=pltpu.CompilerParams(dimension_semantics=("arbitrary",),
                                             vmem_limit_bytes=_VMEM_LIMIT_LARGE),
        name="grouped_ffn",
    )(tile_expert, n_used, xs, w_in, w_out)


def _moe_post_kernel(pos_ref, x_ref, meta_ref, ys_hbm, g_ref, b_ref, p_ref, wg_ref, bg_ref, wp_ref,
                     out_ref, outb_ref, buf, sem, *, alpha, tm, n_tiles):
    i = pl.program_id(0)
    slot = i % 2

    def gather(tile, to_slot):
        base = tile * (TOP_K * tm)

        def issue(r, k):
            src = pos_ref[base + TOP_K * r + k]
            pltpu.make_async_copy(ys_hbm.at[pl.ds(src, 1)], buf.at[to_slot, k, pl.ds(r, 1)],
                                  sem.at[to_slot]).start()

        _for_each_row(tm, issue)

    @pl.when(i == 0)
    def _():
        gather(0, 0)

    @pl.when(i + 1 < n_tiles)
    def _():
        gather(i + 1, 1 - slot)

    for k in range(TOP_K):
        pltpu.make_async_copy(ys_hbm.at[pl.ds(0, tm)], buf.at[slot, k], sem.at[slot]).wait()
    meta = meta_ref[...]
    ff = meta[:, _META_GATE:_META_GATE + 1] * buf[slot, 0]
    for k in range(1, TOP_K):
        ff = ff + meta[:, _META_GATE + k:_META_GATE + k + 1] * buf[slot, k]
    y = _norm_ple(alpha * x_ref[...] + ff, p_ref[...], g_ref, b_ref, wg_ref, bg_ref, wp_ref)
    out_ref[...] = y
    outb_ref[...] = y.astype(_BF)


def _moe_post(x, ys, pos, meta, g, b, p, w_gate, b_gate, w_proj, alpha):
    n, d = x.shape
    pd = p.shape[1]
    tm = _pick(n, 256)
    tile = lambda width: pl.BlockSpec((tm, width), lambda i, pos: (i, 0))
    res = lambda shape: pl.BlockSpec(shape, lambda i, pos: (0,) * len(shape),
                                     pipeline_mode=pl.Buffered(1))
    return pl.pallas_call(
        functools.partial(_moe_post_kernel, alpha=alpha, tm=tm, n_tiles=n // tm),
        grid_spec=pltpu.PrefetchScalarGridSpec(
            num_scalar_prefetch=1,
            grid=(n // tm,),
            in_specs=[tile(d), tile(_LANES), pl.BlockSpec(memory_space=pl.ANY), res((1, d)),
                      res((1, d)), tile(pd), res((d, d)), res((1, d)), res((pd, d))],
            out_specs=[tile(d), tile(d)],
            scratch_shapes=[pltpu.VMEM((2, TOP_K, tm, d), _F32), pltpu.SemaphoreType.DMA((2,))],
        ),
        out_shape=[jax.ShapeDtypeStruct((n, d), _F32), jax.ShapeDtypeStruct((n, d), _BF)],
        compiler_params=_cparams("arbitrary"),
        name="moe_post",
    )(pos, x, meta, ys, g.reshape(1, d), b.reshape(1, d), p, w_gate, b_gate.reshape(1, d), w_proj)


_FUSED_MIXER_MIN_T = 256
_SPARSE_MOE_MIN_TOKENS = 4096
_GROUP_TILE_ROWS = 512


def _trunk(x, p, lru_conv, lru_h, cm_conv, w):
    bsz, t, d = x.shape
    depth = p.shape[0]
    n = bsz * t
    alpha = (2.0 * depth) ** 0.25
    n_mixers = 2
    x = x.reshape(n, d)
    xb = x
    new_lru_conv, new_lru_h, new_cm_conv = [], [], []
    for i in range(depth):
        j = i // n_mixers
        if i % n_mixers == 0:
            lw = w['rglru_w_out'].shape[1]
            gate, rec = _in_proj(xb, w['rglru_w_in'][j])
            core_args = (lru_conv[j], lru_h[j], w['rglru_conv_w'][j], w['rglru_conv_b'][j],
                         w['rglru_w_a'][j], w['rglru_b_a'][j], w['rglru_w_x'][j],
                         w['rglru_b_x'][j], w['rglru_lambda'][j])
            if t >= _FUSED_MIXER_MIN_T:
                x, xb, buf, hl = _rglru_layer(
                    rec.reshape(bsz, t, lw), gate.reshape(bsz, t, lw), x.reshape(bsz, t, d),
                    *core_args, w['rglru_w_out'][j], w['ln_mix_g'][i], w['ln_mix_b'][i], alpha)
                x, xb = x.reshape(n, d), xb.reshape(n, d)
            else:
                y, buf, hl = _rglru(rec.reshape(bsz, t, lw), gate.reshape(bsz, t, lw), *core_args)
                x, xb = _proj_ln(y.reshape(n, lw), w['rglru_w_out'][j], jnp.zeros((d,), _F32), x,
                                 w['ln_mix_g'][i], w['ln_mix_b'][i], alpha)
            new_lru_conv.append(buf)
            new_lru_h.append(hl)
        else:
            if t >= _FUSED_MIXER_MIN_T:
                x, xb, buf = _conv_layer(
                    xb.astype(_BF).reshape(bsz, t, d), x.reshape(bsz, t, d), cm_conv[j],
                    w['cm_w_pw1'][j], w['cm_b_pw1'][j], w['cm_dw_w'][j], w['cm_dw_b'][j],
                    w['cm_ln_g'][j], w['cm_ln_b'][j], w['cm_w_pw2'][j], w['cm_b_pw2'][j],
                    w['ln_mix_g'][i], w['ln_mix_b'][i], alpha)
                x, xb = x.reshape(n, d), xb.reshape(n, d)
            else:
                v = _pw1_glu(xb.astype(_BF), w['cm_w_pw1'][j], w['cm_b_pw1'][j])
                y, buf = _convmod(v.reshape(bsz, t, d), cm_conv[j], w['cm_dw_w'][j],
                                  w['cm_dw_b'][j], w['cm_ln_g'][j], w['cm_ln_b'][j])
                x, xb = _proj_ln(y.reshape(n, d), w['cm_w_pw2'][j], w['cm_b_pw2'][j], x,
                                 w['ln_mix_g'][i], w['ln_mix_b'][i], alpha)
            new_cm_conv.append(buf)
        k = i // 2
        post_args = (w['ln_ffn_g'][i], w['ln_ffn_b'][i], p[i].reshape(n, -1), w['ple_w_gate'][i],
                     w['ple_b_gate'][i], w['ple_w_proj'][i], alpha)
        if i % 2 == 0:
            ff = _ffn(xb, w['ffn_w_in'][k][None], w['ffn_w_out'][k][None])
            x, xb = _post(x, ff, *post_args)
        else:
            gates, meta, counts = _router(xb, w['moe_w_router'][k])
            ne = w['moe_w_router'].shape[-1]
            if n < _SPARSE_MOE_MIN_TOKENS:
                ff = _ffn(xb, w['moe_w_in'][k], w['moe_w_out'][k], gates, tf_pref=1408)
                x, xb = _post(x, ff, *post_args)
            else:
                tg = _GROUP_TILE_ROWS
                pos, ends, tile_expert, n_used, n_tiles = _route_plan(meta, counts, ne, tg)
                xs = _dispatch(x, pos, ends, n_tiles * tg, tg)
                ys = _grouped_ffn(xs, w['moe_w_in'][k], w['moe_w_out'][k], tile_expert, n_used,
                                  tg)
                x, xb = _moe_post(x, ys, pos, meta, *post_args)
    return (x.reshape(bsz, t, d), jnp.stack(new_lru_conv), jnp.stack(new_lru_h),
            jnp.stack(new_cm_conv))


_MATMUL_WEIGHTS = ('rglru_w_in', 'rglru_w_a', 'rglru_w_x', 'rglru_w_out', 'cm_w_pw1', 'cm_w_pw2',
                   'ffn_w_in', 'ffn_w_out', 'moe_w_in', 'moe_w_out', 'ple_w_proj', 'ple_w_gate')


def kernel(x_prompt, x_sample, p_prompt, p_sample, state_rglru_conv, state_rglru_h, state_conv_module, rglru_w_in, rglru_conv_w, rglru_conv_b, rglru_w_a, rglru_b_a, rglru_w_x, rglru_b_x, rglru_lambda, rglru_w_out, cm_w_pw1, cm_b_pw1, cm_dw_w, cm_dw_b, cm_ln_g, cm_ln_b, cm_w_pw2, cm_b_pw2, ffn_w_in, ffn_w_out, moe_w_router, moe_w_in, moe_w_out, ln_mix_g, ln_mix_b, ln_ffn_g, ln_ffn_b, ple_w_proj, ple_w_gate, ple_b_gate):
    w = dict(rglru_w_in=rglru_w_in, rglru_conv_w=rglru_conv_w, rglru_conv_b=rglru_conv_b,
             rglru_w_a=rglru_w_a, rglru_b_a=rglru_b_a, rglru_w_x=rglru_w_x, rglru_b_x=rglru_b_x,
             rglru_lambda=rglru_lambda, rglru_w_out=rglru_w_out, cm_w_pw1=cm_w_pw1,
             cm_b_pw1=cm_b_pw1, cm_dw_w=cm_dw_w, cm_dw_b=cm_dw_b, cm_ln_g=cm_ln_g,
             cm_ln_b=cm_ln_b, cm_w_pw2=cm_w_pw2, cm_b_pw2=cm_b_pw2, ffn_w_in=ffn_w_in,
             ffn_w_out=ffn_w_out, moe_w_router=moe_w_router, moe_w_in=moe_w_in,
             moe_w_out=moe_w_out, ln_mix_g=ln_mix_g, ln_mix_b=ln_mix_b, ln_ffn_g=ln_ffn_g,
             ln_ffn_b=ln_ffn_b, ple_w_proj=ple_w_proj, ple_w_gate=ple_w_gate,
             ple_b_gate=ple_b_gate)
    for name in _MATMUL_WEIGHTS:
        w[name] = w[name].astype(_BF)
    bp = x_prompt.shape[0]
    dt = x_prompt.dtype
    zero_conv = jnp.zeros((state_rglru_conv.shape[0], bp) + state_rglru_conv.shape[2:], dt)
    zero_h = jnp.zeros((state_rglru_h.shape[0], bp) + state_rglru_h.shape[2:], dt)
    zero_cm = jnp.zeros((state_conv_module.shape[0], bp) + state_conv_module.shape[2:], dt)
    y_p, conv_p, h_p, cm_p = _trunk(x_prompt, p_prompt, zero_conv, zero_h, zero_cm, w)
    y_s, conv_s, h_s, cm_s = _trunk(x_sample, p_sample, state_rglru_conv, state_rglru_h,
                                    state_conv_module, w)
    return (y_p, y_s, conv_p, h_p, cm_p, conv_s, h_s, cm_s)
```

```python
import functools

import jax
import jax.numpy as jnp
from jax import lax
from jax.experimental import pallas as pl
from jax.experimental.pallas import tpu as pltpu

_BF = jnp.bfloat16
_F32 = jnp.float32

LRU_C = 8.0
LN_EPS = 1e-5
TOP_K = 2

_V7X_VMEM_BYTES = 64 * 1024 * 1024
_VMEM_LIMIT = _V7X_VMEM_BYTES - 8 * 1024 * 1024
_VMEM_LIMIT_LARGE = _V7X_VMEM_BYTES - 4 * 1024 * 1024
_SUBLANES = 8
_LANES = 128
_V7X_MXU_DIM = 256
_ROW_BLOCK = 128


def _cparams(*sem):
    return pltpu.CompilerParams(dimension_semantics=sem, vmem_limit_bytes=_VMEM_LIMIT)


def _pick(n, pref, mult=_SUBLANES):
    best = None
    for c in range(mult, min(n, pref) + 1, mult):
        if n % c == 0:
            best = c
    assert best is not None, (n, pref, mult)
    return best


def _row_blocks(tm):
    rb = _pick(tm, _ROW_BLOCK)
    return [slice(r0, r0 + rb) for r0 in range(0, tm, rb)]


def _for_each_row(tm, issue):
    def group(g, c):
        r0 = pl.multiple_of(g * _SUBLANES, _SUBLANES)
        for j in range(_SUBLANES):
            for k in range(TOP_K):
                issue(r0 + j, k)
        return c

    lax.fori_loop(0, tm // _SUBLANES, group, 0)


def _resident(shape):
    nd = len(shape)
    return pl.BlockSpec(shape, lambda *_: (0,) * nd, pipeline_mode=pl.Buffered(1))


def _layer_norm(x, g, b):
    mu = jnp.mean(x, axis=-1, keepdims=True)
    xc = x - mu
    var = jnp.mean(xc * xc, axis=-1, keepdims=True)
    return xc * lax.rsqrt(var + LN_EPS) * g + b


def _sigmoid(x):
    return 1.0 / (1.0 + jnp.exp(-x))


def _gelu_tanh(x):
    c = 0.7978845608028654
    return 0.5 * x * (1.0 + jnp.tanh(c * (x + 0.044715 * (x * x * x))))


def _softplus(x):
    return jnp.maximum(x, 0.0) + jnp.log1p(jnp.exp(-jnp.abs(x)))


def _in_proj_kernel(x_ref, wg_ref, wr_ref, gate_ref, rec_ref, xb_scr):
    @pl.when(pl.program_id(1) == 0)
    def _():
        xb_scr[...] = x_ref[...].astype(_BF)

    for rows in _row_blocks(x_ref.shape[0]):
        xb = xb_scr[rows, :]
        gate_ref[rows, :] = _gelu_tanh(jnp.dot(xb, wg_ref[...], preferred_element_type=_F32))
        rec_ref[rows, :] = jnp.dot(xb, wr_ref[...], preferred_element_type=_F32)


def _in_proj(x, w):
    n, d = x.shape
    lw = w.shape[1] // 2
    tm = _pick(n, 1024)
    tn = _pick(lw, 512, _LANES)
    nj = lw // tn
    return pl.pallas_call(
        _in_proj_kernel,
        grid=(n // tm, nj),
        in_specs=[
            pl.BlockSpec((tm, d), lambda i, j: (i, 0)),
            pl.BlockSpec((d, tn), lambda i, j: (0, j)),
            pl.BlockSpec((d, tn), lambda i, j: (0, j + nj)),
        ],
        out_specs=[
            pl.BlockSpec((tm, tn), lambda i, j: (i, j)),
            pl.BlockSpec((tm, tn), lambda i, j: (i, j)),
        ],
        out_shape=[jax.ShapeDtypeStruct((n, lw), _F32)] * 2,
        scratch_shapes=[pltpu.VMEM((tm, d), _BF)],
        compiler_params=_cparams("parallel", "arbitrary"),
        name="in_proj",
    )(x, w, w)


def _rglru_kernel(rec_ref, gate_ref, buf0_ref, h0_ref, cw_ref, cb_ref, wa_ref, ba_ref, wx_ref,
                  bx_ref, lam_ref, y_ref, nbuf_ref, hlast_ref, xp_scr, a_scr, b_scr, carry_scr,
                  *, tc, heads, kw):
    t = pl.program_id(1)
    lw = rec_ref.shape[-1]
    blk = lw // heads
    hist = _SUBLANES
    base = hist - (kw - 1)

    @pl.when(t == 0)
    def _():
        xp_scr[0:hist, :] = buf0_ref[0]
        carry_scr[...] = jnp.broadcast_to(h0_ref[0], (_SUBLANES, lw))

    xp_scr[hist:hist + tc, :] = rec_ref[0]
    decay = -LRU_C * _softplus(-lam_ref[...])
    for h in range(heads):
        sl = slice(h * blk, (h + 1) * blk)
        whole = xp_scr[:, sl]
        xc = jnp.broadcast_to(cb_ref[:, sl], (tc, blk))
        for k in range(kw):
            j = (base + k) % _SUBLANES
            q0 = (base + k) - j
            win = whole if j == 0 else pltpu.roll(whole, hist + tc - j, 0)
            xc = xc + cw_ref[k:k + 1, sl] * win[q0:q0 + tc]
        xcb = xc.astype(_BF)
        r = _sigmoid(jnp.dot(xcb, wa_ref[h], preferred_element_type=_F32) + ba_ref[:, sl])
        ig = _sigmoid(jnp.dot(xcb, wx_ref[h], preferred_element_type=_F32) + bx_ref[:, sl])
        log_a = r * decay[:, sl]
        a = jnp.exp(log_a)
        a_scr[:, sl] = a
        b_scr[:, sl] = jnp.sqrt(-jnp.tanh(log_a) * (a * a + 1.0)) * (ig * xc)

    row = lax.broadcasted_iota(jnp.int32, (_SUBLANES, lw), 0)

    def body(g, carry):
        r0 = pl.multiple_of(g * _SUBLANES, _SUBLANES)
        a = a_scr[pl.ds(r0, _SUBLANES), :]
        b = b_scr[pl.ds(r0, _SUBLANES), :]
        for s in (1, 2, 4):
            m = row >= s
            a_sh = jnp.where(m, pltpu.roll(a, s, 0), 1.0)
            b_sh = jnp.where(m, pltpu.roll(b, s, 0), 0.0)
            b = a * b_sh + b
            a = a * a_sh
        hrows = a * carry + b
        b_scr[pl.ds(r0, _SUBLANES), :] = hrows
        return jnp.broadcast_to(hrows[_SUBLANES - 1:_SUBLANES, :], (_SUBLANES, lw))

    carry = lax.fori_loop(0, tc // _SUBLANES, body, carry_scr[...])
    carry_scr[...] = carry
    y_ref[0] = (b_scr[...] * gate_ref[0]).astype(y_ref.dtype)
    hlast_ref[0] = carry
    tail = xp_scr[tc:tc + hist, :]
    nbuf_ref[0] = tail
    xp_scr[0:hist, :] = tail


def _rglru_layer_kernel(rec_ref, gate_ref, buf0_ref, h0_ref, cw_ref, cb_ref, wa_ref, ba_ref,
                        wx_ref, bx_ref, lam_ref, x_ref, wo_ref, g_ref, b_ref, out_ref, outb_ref,
                        nbuf_ref, hlast_ref, xp_scr, a_scr, b_scr, carry_scr, y_scr, mix_scr,
                        *, tc, heads, kw, n_chunks, n_live, alpha):
    s = pl.program_id(0)
    t = s % n_chunks
    live = s < n_live
    lw = rec_ref.shape[-1]
    blk = lw // heads
    hist = _SUBLANES
    base = hist - (kw - 1)

    @pl.when(s == 0)
    def _():
        y_scr[...] = jnp.zeros_like(y_scr)

    @pl.when((t == 0) & live)
    def _():
        xp_scr[0:hist, :] = buf0_ref[0]
        carry_scr[...] = jnp.broadcast_to(h0_ref[0], (_SUBLANES, lw))

    xp_scr[hist:hist + tc, :] = rec_ref[0]
    d = out_ref.shape[-1]
    dblk = d // heads
    y_prev = y_scr[...]
    decay = -LRU_C * _softplus(-lam_ref[...])
    for h in range(heads):
        cols = slice(h * dblk, (h + 1) * dblk)
        mix_scr[:, cols] = jnp.dot(y_prev, wo_ref[:, cols], preferred_element_type=_F32)
        sl = slice(h * blk, (h + 1) * blk)
        whole = xp_scr[:, sl]
        xc = jnp.broadcast_to(cb_ref[:, sl], (tc, blk))
        for k in range(kw):
            j = (base + k) % _SUBLANES
            q0 = (base + k) - j
            win = whole if j == 0 else pltpu.roll(whole, hist + tc - j, 0)
            xc = xc + cw_ref[k:k + 1, sl] * win[q0:q0 + tc]
        xcb = xc.astype(_BF)
        r = _sigmoid(jnp.dot(xcb, wa_ref[h], preferred_element_type=_F32) + ba_ref[:, sl])
        ig = _sigmoid(jnp.dot(xcb, wx_ref[h], preferred_element_type=_F32) + bx_ref[:, sl])
        log_a = r * decay[:, sl]
        a = jnp.exp(log_a)
        a_scr[:, sl] = a
        b_scr[:, sl] = jnp.sqrt(-jnp.tanh(log_a) * (a * a + 1.0)) * (ig * xc)
    xo = _layer_norm(alpha * x_ref[0] + mix_scr[...], g_ref[...], b_ref[...])
    out_ref[0] = xo
    outb_ref[0] = xo.astype(_BF)

    row = lax.broadcasted_iota(jnp.int32, (_SUBLANES, lw), 0)

    def body(g, carry):
        r0 = pl.multiple_of(g * _SUBLANES, _SUBLANES)
        a = a_scr[pl.ds(r0, _SUBLANES), :]
        b = b_scr[pl.ds(r0, _SUBLANES), :]
        for sh in (1, 2, 4):
            m = row >= sh
            a_sh = jnp.where(m, pltpu.roll(a, sh, 0), 1.0)
            b_sh = jnp.where(m, pltpu.roll(b, sh, 0), 0.0)
            b = a * b_sh + b
            a = a * a_sh
        hrows = a * carry + b
        b_scr[pl.ds(r0, _SUBLANES), :] = hrows
        return jnp.broadcast_to(hrows[_SUBLANES - 1:_SUBLANES, :], (_SUBLANES, lw))

    carry = lax.fori_loop(0, tc // _SUBLANES, body, carry_scr[...])
    y_scr[...] = (b_scr[...] * gate_ref[0]).astype(_BF)

    @pl.when(live)
    def _():
        carry_scr[...] = carry
        hlast_ref[0] = carry
        tail = xp_scr[tc:tc + hist, :]
        nbuf_ref[0] = tail
        xp_scr[0:hist, :] = tail


def _rglru_layer(rec, gate, x, buf0, h0, conv_w, conv_b, w_a, b_a, w_x, b_x, lam, w_out, ln_g,
                 ln_b, alpha):
    bsz, t, lw = rec.shape
    d = x.shape[-1]
    heads, blk, _ = w_a.shape
    kw = conv_w.shape[0]
    assert kw - 1 <= _SUBLANES <= t and t % _SUBLANES == 0
    tc = _pick(t, 256)
    n_chunks = t // tc
    n_live = bsz * n_chunks
    buf_pad = jnp.pad(buf0, ((0, 0), (_SUBLANES - (kw - 1), 0), (0, 0)))
    row = lambda v: v.reshape(1, -1)

    def cur(s):
        c = jnp.minimum(s, n_live - 1)
        return c // n_chunks, c % n_chunks

    def prev(s):
        c = jnp.maximum(s - 1, 0)
        return c // n_chunks, c % n_chunks

    cur_seq = pl.BlockSpec((1, tc, lw), lambda s: (*cur(s), 0))
    prev_seq = pl.BlockSpec((1, tc, d), lambda s: (*prev(s), 0))
    per_b = pl.BlockSpec((1, _SUBLANES, lw), lambda s: (cur(s)[0], 0, 0))
    out, outb, nbuf, hlast = pl.pallas_call(
        functools.partial(_rglru_layer_kernel, tc=tc, heads=heads, kw=kw, n_chunks=n_chunks,
                          n_live=n_live, alpha=alpha),
        grid=(n_live + 1,),
        in_specs=[
            cur_seq, cur_seq, per_b,
            pl.BlockSpec((1, 1, lw), lambda s: (cur(s)[0], 0, 0)),
            _resident((kw, lw)), _resident((1, lw)),
            _resident((heads, blk, blk)), _resident((1, lw)),
            _resident((heads, blk, blk)), _resident((1, lw)),
            _resident((1, lw)),
            prev_seq, _resident((lw, d)), _resident((1, d)), _resident((1, d)),
        ],
        out_specs=[prev_seq, prev_seq, per_b, per_b],
        out_shape=[
            jax.ShapeDtypeStruct((bsz, t, d), _F32),
            jax.ShapeDtypeStruct((bsz, t, d), _BF),
            jax.ShapeDtypeStruct((bsz, _SUBLANES, lw), _F32),
            jax.ShapeDtypeStruct((bsz, _SUBLANES, lw), _F32),
        ],
        scratch_shapes=[
            pltpu.VMEM((_SUBLANES + tc, lw), _F32),
            pltpu.VMEM((tc, lw), _F32),
            pltpu.VMEM((tc, lw), _F32),
            pltpu.VMEM((_SUBLANES, lw), _F32),
            pltpu.VMEM((tc, lw), _BF),
            pltpu.VMEM((tc, d), _F32),
        ],
        compiler_params=_cparams("arbitrary"),
        name="rglru_layer",
    )(rec, gate, buf_pad, h0.reshape(bsz, 1, lw), conv_w, row(conv_b), w_a, row(b_a), w_x,
      row(b_x), row(lam), x, w_out, row(ln_g), row(ln_b))
    return out, outb, nbuf[:, _SUBLANES - (kw - 1):], hlast[:, 0]


def _rglru(rec, gate, buf0, h0, conv_w, conv_b, w_a, b_a, w_x, b_x, lam):
    bsz, t, lw = rec.shape
    heads, blk, _ = w_a.shape
    kw = conv_w.shape[0]
    assert kw - 1 <= _SUBLANES <= t and t % _SUBLANES == 0
    tc = _pick(t, 256)
    buf_pad = jnp.pad(buf0, ((0, 0), (_SUBLANES - (kw - 1), 0), (0, 0)))
    row = lambda v: v.reshape(1, lw)
    seq = pl.BlockSpec((1, tc, lw), lambda b, i: (b, i, 0))
    per_b = pl.BlockSpec((1, _SUBLANES, lw), lambda b, i: (b, 0, 0))
    y, nbuf, hlast = pl.pallas_call(
        functools.partial(_rglru_kernel, tc=tc, heads=heads, kw=kw),
        grid=(bsz, t // tc),
        in_specs=[
            seq, seq, per_b,
            pl.BlockSpec((1, 1, lw), lambda b, i: (b, 0, 0)),
            _resident((kw, lw)), _resident((1, lw)),
            _resident((heads, blk, blk)), _resident((1, lw)),
            _resident((heads, blk, blk)), _resident((1, lw)),
            _resident((1, lw)),
        ],
        out_specs=[seq, per_b, per_b],
        out_shape=[
            jax.ShapeDtypeStruct((bsz, t, lw), _BF),
            jax.ShapeDtypeStruct((bsz, _SUBLANES, lw), _F32),
            jax.ShapeDtypeStruct((bsz, _SUBLANES, lw), _F32),
        ],
        scratch_shapes=[
            pltpu.VMEM((_SUBLANES + tc, lw), _F32),
            pltpu.VMEM((tc, lw), _F32),
            pltpu.VMEM((tc, lw), _F32),
            pltpu.VMEM((_SUBLANES, lw), _F32),
        ],
        compiler_params=_cparams("parallel", "arbitrary"),
        name="rglru",
    )(rec, gate, buf_pad, h0.reshape(bsz, 1, lw), conv_w, row(conv_b), w_a, row(b_a), w_x,
      row(b_x), row(lam))
    return y, nbuf[:, _SUBLANES - (kw - 1):], hlast[:, 0]


def _proj_ln_kernel(a_ref, w_ref, bias_ref, x_ref, g_ref, b_ref, out_ref, outb_ref, *, alpha):
    for rows in _row_blocks(a_ref.shape[0]):
        mix = jnp.dot(a_ref[rows, :], w_ref[...], preferred_element_type=_F32) + bias_ref[...]
        y = _layer_norm(alpha * x_ref[rows, :] + mix, g_ref[...], b_ref[...])
        out_ref[rows, :] = y
        outb_ref[rows, :] = y.astype(_BF)


def _proj_ln(a, w, bias, x, g, b, alpha):
    n, k = a.shape
    d = w.shape[1]
    tm = _pick(n, 512)
    tile = lambda width: pl.BlockSpec((tm, width), lambda i: (i, 0))
    return pl.pallas_call(
        functools.partial(_proj_ln_kernel, alpha=alpha),
        grid=(n // tm,),
        in_specs=[tile(k), _resident((k, d)), _resident((1, d)), tile(d), _resident((1, d)),
                  _resident((1, d))],
        out_specs=[tile(d), tile(d)],
        out_shape=[jax.ShapeDtypeStruct((n, d), _F32), jax.ShapeDtypeStruct((n, d), _BF)],
        compiler_params=_cparams("parallel"),
        name="proj_ln",
    )(a, w, bias.reshape(1, d), x, g.reshape(1, d), b.reshape(1, d))


def _ffn_kernel(x_ref, wg_ref, wu_ref, wo_ref, *rest, gated):
    if gated:
        gates_ref, out_ref = rest
    else:
        (out_ref,) = rest
    e = pl.program_id(1)
    f = pl.program_id(2)

    @pl.when((e == 0) & (f == 0))
    def _():
        out_ref[...] = jnp.zeros_like(out_ref)

    xb = x_ref[...]
    g = jnp.dot(xb, wg_ref[0], preferred_element_type=_F32)
    u = jnp.dot(xb, wu_ref[0], preferred_element_type=_F32)
    h = (g * _sigmoid(g)) * u
    if gated:
        lane = lax.broadcasted_iota(jnp.int32, gates_ref.shape, 1)
        h = h * jnp.sum(jnp.where(lane == e, gates_ref[...], 0.0), axis=1, keepdims=True)
    out_ref[...] += jnp.dot(h.astype(_BF), wo_ref[0], preferred_element_type=_F32)


def _ffn(xb, w_in, w_out, gates=None, tf_pref=512):
    n, d = xb.shape
    ne, f, _ = w_out.shape
    tm = _pick(n, 1024)
    tf = _pick(f, tf_pref, _LANES)
    nf = f // tf
    in_specs = [
        pl.BlockSpec((tm, d), lambda i, e, j: (i, 0)),
        pl.BlockSpec((1, d, tf), lambda i, e, j: (e, 0, j)),
        pl.BlockSpec((1, d, tf), lambda i, e, j: (e, 0, j + nf)),
        pl.BlockSpec((1, tf, d), lambda i, e, j: (e, j, 0)),
    ]
    args = [xb, w_in, w_in, w_out]
    if gates is not None:
        in_specs.append(pl.BlockSpec((tm, _LANES), lambda i, e, j: (i, 0)))
        args.append(gates)
    return pl.pallas_call(
        functools.partial(_ffn_kernel, gated=gates is not None),
        grid=(n // tm, ne, nf),
        in_specs=in_specs,
        out_specs=pl.BlockSpec((tm, d), lambda i, e, j: (i, 0)),
        out_shape=jax.ShapeDtypeStruct((n, d), _F32),
        compiler_params=_cparams("parallel", "arbitrary", "arbitrary"),
        name="ffn",
    )(*args)


def _norm_ple(pre, p, g_ref, b_ref, wg_ref, bg_ref, wp_ref):
    xn = _layer_norm(pre, g_ref[...], b_ref[...])
    gate = _sigmoid(jnp.dot(xn.astype(_BF), wg_ref[...], preferred_element_type=_F32)
                    + bg_ref[...])
    proj = jnp.dot(p.astype(_BF), wp_ref[...], preferred_element_type=_F32)
    return xn + gate * proj


def _post_kernel(x_ref, ff_ref, g_ref, b_ref, p_ref, wg_ref, bg_ref, wp_ref, out_ref, outb_ref,
                 *, alpha):
    y = _norm_ple(alpha * x_ref[...] + ff_ref[...], p_ref[...], g_ref, b_ref, wg_ref, bg_ref,
                  wp_ref)
    out_ref[...] = y
    outb_ref[...] = y.astype(_BF)


def _post(x, ff, g, b, p, w_gate, b_gate, w_proj, alpha):
    n, d = x.shape
    pd = p.shape[1]
    tm = _pick(n, 512)
    tile = lambda width: pl.BlockSpec((tm, width), lambda i: (i, 0))
    return pl.pallas_call(
        functools.partial(_post_kernel, alpha=alpha),
        grid=(n // tm,),
        in_specs=[tile(d), tile(d), _resident((1, d)), _resident((1, d)), tile(pd),
                  _resident((d, d)), _resident((1, d)), _resident((pd, d))],
        out_specs=[tile(d), tile(d)],
        out_shape=[jax.ShapeDtypeStruct((n, d), _F32), jax.ShapeDtypeStruct((n, d), _BF)],
        compiler_params=_cparams("parallel"),
        name="post",
    )(x, ff, g.reshape(1, d), b.reshape(1, d), p, w_gate, b_gate.reshape(1, d), w_proj)


def _pw1_glu_kernel(x_ref, wv_ref, wg_ref, bv_ref, bg_ref, v_ref):
    for rows in _row_blocks(x_ref.shape[0]):
        xb = x_ref[rows, :]
        val = jnp.dot(xb, wv_ref[...], preferred_element_type=_F32) + bv_ref[...]
        gate = jnp.dot(xb, wg_ref[...], preferred_element_type=_F32) + bg_ref[...]
        v_ref[rows, :] = val * _sigmoid(gate)


def _pw1_glu(xb, w, bias):
    n, d = xb.shape
    dv = w.shape[1] // 2
    tm = _pick(n, 1024)
    tn = _pick(dv, 512, _LANES)
    nj = dv // tn
    bias = bias.reshape(1, 2 * dv)
    return pl.pallas_call(
        _pw1_glu_kernel,
        grid=(n // tm, nj),
        in_specs=[
            pl.BlockSpec((tm, d), lambda i, j: (i, 0)),
            pl.BlockSpec((d, tn), lambda i, j: (0, j)),
            pl.BlockSpec((d, tn), lambda i, j: (0, j + nj)),
            pl.BlockSpec((1, tn), lambda i, j: (0, j)),
            pl.BlockSpec((1, tn), lambda i, j: (0, j + nj)),
        ],
        out_specs=pl.BlockSpec((tm, tn), lambda i, j: (i, j)),
        out_shape=jax.ShapeDtypeStruct((n, dv), _F32),
        compiler_params=_cparams("parallel", "arbitrary"),
        name="pw1_glu",
    )(xb, w, w, bias, bias)


def _convmod_kernel(v_ref, buf0_ref, w_ref, b_ref, g_ref, beta_ref, y_ref, nbuf_ref, vp_scr,
                    c_scr, *, tc, ksz, hist):
    t = pl.program_id(1)
    d = v_ref.shape[-1]
    base = hist - (ksz - 1)

    @pl.when(t == 0)
    def _():
        vp_scr[0:hist, :] = buf0_ref[0]

    vp_scr[hist:hist + tc, :] = v_ref[0]
    _depthwise_conv(vp_scr, 0, tc, hist, w_ref, b_ref, ksz, c_scr)
    z = _layer_norm(c_scr[...], g_ref[...], beta_ref[...])
    y_ref[0] = (z * _sigmoid(z)).astype(y_ref.dtype)
    tail = vp_scr[tc:tc + hist, :]
    nbuf_ref[0] = tail
    vp_scr[0:hist, :] = tail


def _depthwise_conv(vp_scr, r0, nrows, hist, w_ref, b_ref, ksz, c_scr):
    d = vp_scr.shape[-1]
    base = hist - (ksz - 1)
    rows = hist + nrows
    for c in range(d // _LANES):
        sl = slice(c * _LANES, (c + 1) * _LANES)
        acc = jnp.broadcast_to(b_ref[:, sl], (nrows, _LANES))
        whole = vp_scr[r0:r0 + rows, sl]
        for j in range(_SUBLANES):
            taps = [k for k in range(ksz) if (base + k) % _SUBLANES == j]
            if not taps:
                continue
            win = whole if j == 0 else pltpu.roll(whole, rows - j, 0)
            for k in taps:
                q0 = ((base + k) // _SUBLANES) * _SUBLANES
                acc = acc + w_ref[k:k + 1, sl] * win[q0:q0 + nrows]
        c_scr[r0:r0 + nrows, sl] = acc


def _conv_layer_kernel(xb_ref, x_ref, buf0_ref, w1_ref, b1_ref, dw_ref, db_ref, cg_ref, cb_ref,
                       w2_ref, b2_ref, g_ref, b_ref, out_ref, outb_ref, nbuf_ref, vp_scr, c_scr,
                       *, tc, sub, ksz, hist, alpha):
    d = x_ref.shape[-1]

    @pl.when(pl.program_id(1) == 0)
    def _():
        vp_scr[0:hist, :] = buf0_ref[0]

    for r0 in range(0, tc, sub):
        rows = slice(r0, r0 + sub)
        xb = xb_ref[0, rows, :]
        val = jnp.dot(xb, w1_ref[:, :d], preferred_element_type=_F32) + b1_ref[:, :d]
        gate = jnp.dot(xb, w1_ref[:, d:], preferred_element_type=_F32) + b1_ref[:, d:]
        vp_scr[hist + r0:hist + r0 + sub, :] = val * _sigmoid(gate)
        _depthwise_conv(vp_scr, r0, sub, hist, dw_ref, db_ref, ksz, c_scr)
        z = _layer_norm(c_scr[rows, :], cg_ref[...], cb_ref[...])
        y = (z * _sigmoid(z)).astype(_BF)
        mix = jnp.dot(y, w2_ref[...], preferred_element_type=_F32) + b2_ref[...]
        xo = _layer_norm(alpha * x_ref[0, rows, :] + mix, g_ref[...], b_ref[...])
        out_ref[0, rows, :] = xo
        outb_ref[0, rows, :] = xo.astype(_BF)
    tail = vp_scr[tc:tc + hist, :]
    nbuf_ref[0] = tail
    vp_scr[0:hist, :] = tail


def _conv_layer(xb, x, buf0, w_pw1, b_pw1, dw_w, dw_b, cm_g, cm_b, w_pw2, b_pw2, ln_g, ln_b,
                alpha):
    bsz, t, d = x.shape
    ksz = dw_w.shape[0]
    hist = -(-(ksz - 1) // _SUBLANES) * _SUBLANES
    assert t % _SUBLANES == 0 and d % _LANES == 0
    tc = _pick(t, 256)
    sub = _pick(tc, 128)
    buf_pad = jnp.pad(buf0, ((0, 0), (hist - (ksz - 1), 0), (0, 0)))
    row = lambda a: a.reshape(1, -1)
    seq = pl.BlockSpec((1, tc, d), lambda b, i: (b, i, 0))
    per_b = pl.BlockSpec((1, hist, d), lambda b, i: (b, 0, 0))
    out, outb, nbuf = pl.pallas_call(
        functools.partial(_conv_layer_kernel, tc=tc, sub=sub, ksz=ksz, hist=hist, alpha=alpha),
        grid=(bsz, t // tc),
        in_specs=[seq, seq, per_b, _resident((d, 2 * d)), _resident((1, 2 * d)),
                  _resident((ksz, d)), _resident((1, d)), _resident((1, d)), _resident((1, d)),
                  _resident((d, d)), _resident((1, d)), _resident((1, d)), _resident((1, d))],
        out_specs=[seq, seq, per_b],
        out_shape=[jax.ShapeDtypeStruct((bsz, t, d), _F32), jax.ShapeDtypeStruct((bsz, t, d), _BF),
                   jax.ShapeDtypeStruct((bsz, hist, d), _F32)],
        scratch_shapes=[pltpu.VMEM((hist + tc, d), _F32), pltpu.VMEM((tc, d), _F32)],
        compiler_params=_cparams("parallel", "arbitrary"),
        name="conv_layer",
    )(xb, x, buf_pad, w_pw1, row(b_pw1), dw_w, row(dw_b), row(cm_g), row(cm_b), w_pw2,
      row(b_pw2), row(ln_g), row(ln_b))
    return out, outb, nbuf[:, hist - (ksz - 1):]


def _convmod(v, buf0, dw_w, dw_b, ln_g, ln_b):
    bsz, t, d = v.shape
    ksz = dw_w.shape[0]
    hist = -(-(ksz - 1) // _SUBLANES) * _SUBLANES
    assert t % _SUBLANES == 0 and d % _LANES == 0
    tc = _pick(t, 128)
    buf_pad = jnp.pad(buf0, ((0, 0), (hist - (ksz - 1), 0), (0, 0)))
    row = lambda a: a.reshape(1, d)
    seq = pl.BlockSpec((1, tc, d), lambda b, i: (b, i, 0))
    per_b = pl.BlockSpec((1, hist, d), lambda b, i: (b, 0, 0))
    y, nbuf = pl.pallas_call(
        functools.partial(_convmod_kernel, tc=tc, ksz=ksz, hist=hist),
        grid=(bsz, t // tc),
        in_specs=[seq, per_b, _resident((ksz, d)), _resident((1, d)), _resident((1, d)),
                  _resident((1, d))],
        out_specs=[seq, per_b],
        out_shape=[jax.ShapeDtypeStruct((bsz, t, d), _BF),
                   jax.ShapeDtypeStruct((bsz, hist, d), _F32)],
        scratch_shapes=[pltpu.VMEM((hist + tc, d), _F32), pltpu.VMEM((tc, d), _F32)],
        compiler_params=_cparams("parallel", "arbitrary"),
        name="convmod",
    )(v, buf_pad, dw_w, row(dw_b), row(ln_g), row(ln_b))
    return y, nbuf[:, hist - (ksz - 1):]


_META_EXPERT, _META_GATE, _META_RANK = 0, TOP_K, 2 * TOP_K


def _router_kernel(xb_ref, w_ref, gates_ref, meta_ref, counts_ref, run_scr, *, n_experts):
    @pl.when(pl.program_id(0) == 0)
    def _():
        run_scr[...] = jnp.zeros_like(run_scr)

    logits = jnp.dot(xb_ref[...], w_ref[...], preferred_element_type=_F32)
    tm = logits.shape[0]
    lane = lax.broadcasted_iota(jnp.int32, logits.shape, 1)
    valid = lane < n_experts
    logits = jnp.where(valid, logits, -1e30)
    ex = jnp.where(valid, jnp.exp(logits - jnp.max(logits, axis=1, keepdims=True)), 0.0)
    probs = ex / jnp.sum(ex, axis=1, keepdims=True)
    rest = jnp.where(valid, probs, -1.0)
    total = jnp.zeros_like(probs[:, :1])
    picks = []
    for _ in range(TOP_K):
        top = jnp.max(rest, axis=1, keepdims=True)
        idx = jnp.min(jnp.where(rest == top, lane, _LANES), axis=1, keepdims=True)
        hit = lane == idx
        picks.append((top, idx, hit))
        total = total + top
        rest = jnp.where(hit, -1.0, rest)
    gates = jnp.zeros_like(probs)
    chosen = jnp.zeros_like(probs)
    for top, _, hit in picks:
        gates = jnp.where(hit, top / total, gates)
        chosen = jnp.where(hit, 1.0, chosen)
    gates_ref[...] = gates
    r_i = lax.broadcasted_iota(jnp.int32, (tm, tm), 0)
    c_i = lax.broadcasted_iota(jnp.int32, (tm, tm), 1)
    tri = jnp.where(r_i > c_i, 1.0, 0.0).astype(_BF)
    before = jnp.dot(tri, chosen.astype(_BF), preferred_element_type=_F32) + run_scr[...]
    meta = jnp.zeros_like(probs)
    for k, (top, idx, hit) in enumerate(picks):
        rank = jnp.sum(jnp.where(hit, before, 0.0), axis=1, keepdims=True)
        meta = jnp.where(lane == _META_EXPERT + k, idx.astype(_F32), meta)
        meta = jnp.where(lane == _META_GATE + k, top / total, meta)
        meta = jnp.where(lane == _META_RANK + k, rank, meta)
    meta_ref[...] = meta
    run_scr[...] += jnp.sum(chosen, axis=0, keepdims=True)
    counts_ref[...] = run_scr[...]


def _router(xb, w_router):
    n, d = xb.shape
    ne = w_router.shape[1]
    assert ne <= _LANES and 3 * TOP_K <= _LANES
    wpad = jnp.pad(w_router, ((0, 0), (0, _LANES - ne))).astype(_BF)
    tm = _pick(n, 512)
    tile = pl.BlockSpec((tm, _LANES), lambda i: (i, 0))
    return pl.pallas_call(
        functools.partial(_router_kernel, n_experts=ne),
        grid=(n // tm,),
        in_specs=[pl.BlockSpec((tm, d), lambda i: (i, 0)), _resident((d, _LANES))],
        out_specs=[tile, tile, pl.BlockSpec((1, _LANES), lambda i: (0, 0))],
        out_shape=[jax.ShapeDtypeStruct((n, _LANES), _F32), jax.ShapeDtypeStruct((n, _LANES), _F32),
                   jax.ShapeDtypeStruct((1, _LANES), _F32)],
        scratch_shapes=[pltpu.VMEM((1, _LANES), _F32)],
        compiler_params=_cparams("arbitrary"),
        name="router",
    )(xb, wpad)


def _route_plan(meta, counts_row, n_experts, tg):
    n = meta.shape[0]
    expert = meta[:, _META_EXPERT:_META_EXPERT + TOP_K].astype(jnp.int32)
    rank = meta[:, _META_RANK:_META_RANK + TOP_K].astype(jnp.int32)
    counts = counts_row[0, :n_experts].astype(jnp.int32)
    padded = (counts + tg - 1) // tg * tg
    ends = jnp.cumsum(padded)
    starts = ends - padded
    pos = (jnp.take(starts, expert) + rank).reshape(-1)
    n_tiles = -(-(TOP_K * n + n_experts * (tg - 1)) // tg)
    tile_start = jnp.arange(n_tiles, dtype=jnp.int32) * tg
    tile_expert = jnp.sum((tile_start[:, None] >= ends[None, :]).astype(jnp.int32), axis=1)
    tile_expert = jnp.minimum(tile_expert, n_experts - 1)
    n_used = (ends[-1:] // tg).astype(jnp.int32)
    return pos, ends, tile_expert, n_used, n_tiles


def _dispatch_kernel(pos_ref, ends_ref, x_ref, out_hbm, sem, *, tm, tg, n_experts):
    base = pl.program_id(0) * (TOP_K * tm)

    def fill_tile(row0):
        for c in range(tg // tm):
            rows = pl.ds(pl.multiple_of(row0 + c * tm, tm), tm)
            fill = pltpu.make_async_copy(x_ref, out_hbm.at[rows], sem)
            fill.start()
            fill.wait()

    @pl.when(pl.program_id(0) == 0)
    def _():
        for e in range(n_experts):
            prev_end = ends_ref[e - 1] if e else 0

            @pl.when(ends_ref[e] > prev_end)
            def _():
                fill_tile(ends_ref[e] - tg)

        for u in range(1, min(n_experts, out_hbm.shape[0] // tg) + 1):
            row0 = out_hbm.shape[0] - u * tg

            @pl.when(row0 >= ends_ref[n_experts - 1])
            def _():
                fill_tile(row0)

    def issue(r, k):
        dst = pos_ref[base + TOP_K * r + k]
        pltpu.make_async_copy(x_ref.at[pl.ds(r, 1)], out_hbm.at[pl.ds(dst, 1)],
                              sem).start(priority=k % 2)

    _for_each_row(tm, issue)
    for k in range(TOP_K):
        pltpu.make_async_copy(x_ref, out_hbm.at[pl.ds(0, tm)], sem).wait()


def _dispatch(x, pos, ends, n_rows, tg):
    n, d = x.shape
    tm = _pick(n, 256)
    assert tg % tm == 0
    return pl.pallas_call(
        functools.partial(_dispatch_kernel, tm=tm, tg=tg, n_experts=ends.shape[0]),
        grid_spec=pltpu.PrefetchScalarGridSpec(
            num_scalar_prefetch=2,
            grid=(n // tm,),
            in_specs=[pl.BlockSpec((tm, d), lambda i, pos, ends: (i, 0))],
            out_specs=pl.BlockSpec(memory_space=pl.ANY),
            scratch_shapes=[pltpu.SemaphoreType.DMA(())],
        ),
        out_shape=jax.ShapeDtypeStruct((n_rows, d), x.dtype),
        compiler_params=_cparams("arbitrary"),
        name="dispatch",
    )(pos, ends, x)


def _grouped_ffn_kernel(te_ref, nu_ref, x_ref, wi_ref, wo_ref, out_ref, *, chunk):
    del te_ref
    f = wo_ref.shape[1]

    @pl.when(pl.program_id(0) < nu_ref[0])
    def _():
        xb = x_ref[...].astype(_BF)
        for c0 in range(0, f, chunk):
            c1 = min(c0 + chunk, f)
            g = jnp.dot(xb, wi_ref[0, :, c0:c1], preferred_element_type=_F32)
            u = jnp.dot(xb, wi_ref[0, :, f + c0:f + c1], preferred_element_type=_F32)
            h = ((g * _sigmoid(g)) * u).astype(_BF)
            part = jnp.dot(h, wo_ref[0, c0:c1, :], preferred_element_type=_F32)
            if c0 == 0:
                out_ref[...] = part
            else:
                out_ref[...] += part

    @pl.when(pl.program_id(0) >= nu_ref[0])
    def _():
        out_ref[...] = jnp.zeros_like(out_ref)


def _grouped_ffn(xs, w_in, w_out, tile_expert, n_used, tg):
    r, d = xs.shape
    f = w_out.shape[1]
    expert_block = lambda shape: pl.BlockSpec(shape, lambda i, te, nu: (te[i], 0, 0),
                                              pipeline_mode=pl.Buffered(1))
    return pl.pallas_call(
        functools.partial(_grouped_ffn_kernel, chunk=2 * _V7X_MXU_DIM),
        grid_spec=pltpu.PrefetchScalarGridSpec(
            num_scalar_prefetch=2,
            grid=(r // tg,),
            in_specs=[
                pl.BlockSpec((tg, d), lambda i, te, nu: (jnp.minimum(i, nu[0] - 1), 0)),
                expert_block((1, d, 2 * f)),
                expert_block((1, f, d)),
            ],
            out_specs=pl.BlockSpec((tg, d), lambda i, te, nu: (i, 0)),
        ),
        out_shape=jax.ShapeDtypeStruct((r, d), _F32),
        compiler_params=pltpu.CompilerParams(dimension_semantics=("arbitrary",),
                                             vmem_limit_bytes=_VMEM_LIMIT_LARGE),
        name="grouped_ffn",
    )(tile_expert, n_used, xs, w_in, w_out)


def _moe_post_kernel(pos_ref, x_ref, meta_ref, ys_hbm, g_ref, b_ref, p_ref, wg_ref, bg_ref, wp_ref,
                     out_ref, outb_ref, buf, sem, *, alpha, tm, n_tiles):
    i = pl.program_id(0)
    slot = i % 2

    def gather(tile, to_slot):
        base = tile * (TOP_K * tm)

        def issue(r, k):
            src = pos_ref[base + TOP_K * r + k]
            pltpu.make_async_copy(ys_hbm.at[pl.ds(src, 1)], buf.at[to_slot, k, pl.ds(r, 1)],
                                  sem.at[to_slot]).start()

        _for_each_row(tm, issue)

    @pl.when(i == 0)
    def _():
        gather(0, 0)

    @pl.when(i + 1 < n_tiles)
    def _():
        gather(i + 1, 1 - slot)

    for k in range(TOP_K):
        pltpu.make_async_copy(ys_hbm.at[pl.ds(0, tm)], buf.at[slot, k], sem.at[slot]).wait()
    meta = meta_ref[...]
    ff = meta[:, _META_GATE:_META_GATE + 1] * buf[slot, 0]
    for k in range(1, TOP_K):
        ff = ff + meta[:, _META_GATE + k:_META_GATE + k + 1] * buf[slot, k]
    y = _norm_ple(alpha * x_ref[...] + ff, p_ref[...], g_ref, b_ref, wg_ref, bg_ref, wp_ref)
    out_ref[...] = y
    outb_ref[...] = y.astype(_BF)


def _moe_post(x, ys, pos, meta, g, b, p, w_gate, b_gate, w_proj, alpha):
    n, d = x.shape
    pd = p.shape[1]
    tm = _pick(n, 256)
    tile = lambda width: pl.BlockSpec((tm, width), lambda i, pos: (i, 0))
    res = lambda shape: pl.BlockSpec(shape, lambda i, pos: (0,) * len(shape),
                                     pipeline_mode=pl.Buffered(1))
    return pl.pallas_call(
        functools.partial(_moe_post_kernel, alpha=alpha, tm=tm, n_tiles=n // tm),
        grid_spec=pltpu.PrefetchScalarGridSpec(
            num_scalar_prefetch=1,
            grid=(n // tm,),
            in_specs=[tile(d), tile(_LANES), pl.BlockSpec(memory_space=pl.ANY), res((1, d)),
                      res((1, d)), tile(pd), res((d, d)), res((1, d)), res((pd, d))],
            out_specs=[tile(d), tile(d)],
            scratch_shapes=[pltpu.VMEM((2, TOP_K, tm, d), _F32), pltpu.SemaphoreType.DMA((2,))],
        ),
        out_shape=[jax.ShapeDtypeStruct((n, d), _F32), jax.ShapeDtypeStruct((n, d), _BF)],
        compiler_params=_cparams("arbitrary"),
        name="moe_post",
    )(pos, x, meta, ys, g.reshape(1, d), b.reshape(1, d), p, w_gate, b_gate.reshape(1, d), w_proj)


_FUSED_MIXER_MIN_T = 256
_SPARSE_MOE_MIN_TOKENS = 4096
_GROUP_TILE_ROWS = 512


def _trunk(x, p, lru_conv, lru_h, cm_conv, w):
    bsz, t, d = x.shape
    depth = p.shape[0]
    n = bsz * t
    alpha = (2.0 * depth) ** 0.25
    n_mixers = 2
    x = x.reshape(n, d)
    xb = x
    new_lru_conv, new_lru_h, new_cm_conv = [], [], []
    for i in range(depth):
        j = i // n_mixers
        if i % n_mixers == 0:
            lw = w['rglru_w_out'].shape[1]
            gate, rec = _in_proj(xb, w['rglru_w_in'][j])
            core_args = (lru_conv[j], lru_h[j], w['rglru_conv_w'][j], w['rglru_conv_b'][j],
                         w['rglru_w_a'][j], w['rglru_b_a'][j], w['rglru_w_x'][j],
                         w['rglru_b_x'][j], w['rglru_lambda'][j])
            if t >= _FUSED_MIXER_MIN_T:
                x, xb, buf, hl = _rglru_layer(
                    rec.reshape(bsz, t, lw), gate.reshape(bsz, t, lw), x.reshape(bsz, t, d),
                    *core_args, w['rglru_w_out'][j], w['ln_mix_g'][i], w['ln_mix_b'][i], alpha)
                x, xb = x.reshape(n, d), xb.reshape(n, d)
            else:
                y, buf, hl = _rglru(rec.reshape(bsz, t, lw), gate.reshape(bsz, t, lw), *core_args)
                x, xb = _proj_ln(y.reshape(n, lw), w['rglru_w_out'][j], jnp.zeros((d,), _F32), x,
                                 w['ln_mix_g'][i], w['ln_mix_b'][i], alpha)
            new_lru_conv.append(buf)
            new_lru_h.append(hl)
        else:
            if t >= _FUSED_MIXER_MIN_T:
                x, xb, buf = _conv_layer(
                    xb.astype(_BF).reshape(bsz, t, d), x.reshape(bsz, t, d), cm_conv[j],
                    w['cm_w_pw1'][j], w['cm_b_pw1'][j], w['cm_dw_w'][j], w['cm_dw_b'][j],
                    w['cm_ln_g'][j], w['cm_ln_b'][j], w['cm_w_pw2'][j], w['cm_b_pw2'][j],
                    w['ln_mix_g'][i], w['ln_mix_b'][i], alpha)
                x, xb = x.reshape(n, d), xb.reshape(n, d)
            else:
                v = _pw1_glu(xb.astype(_BF), w['cm_w_pw1'][j], w['cm_b_pw1'][j])
                y, buf = _convmod(v.reshape(bsz, t, d), cm_conv[j], w['cm_dw_w'][j],
                                  w['cm_dw_b'][j], w['cm_ln_g'][j], w['cm_ln_b'][j])
                x, xb = _proj_ln(y.reshape(n, d), w['cm_w_pw2'][j], w['cm_b_pw2'][j], x,
                                 w['ln_mix_g'][i], w['ln_mix_b'][i], alpha)
            new_cm_conv.append(buf)
        k = i // 2
        post_args = (w['ln_ffn_g'][i], w['ln_ffn_b'][i], p[i].reshape(n, -1), w['ple_w_gate'][i],
                     w['ple_b_gate'][i], w['ple_w_proj'][i], alpha)
        if i % 2 == 0:
            ff = _ffn(xb, w['ffn_w_in'][k][None], w['ffn_w_out'][k][None])
            x, xb = _post(x, ff, *post_args)
        else:
            gates, meta, counts = _router(xb, w['moe_w_router'][k])
            ne = w['moe_w_router'].shape[-1]
            if n < _SPARSE_MOE_MIN_TOKENS:
                ff = _ffn(xb, w['moe_w_in'][k], w['moe_w_out'][k], gates, tf_pref=1408)
                x, xb = _post(x, ff, *post_args)
            else:
                tg = _GROUP_TILE_ROWS
                pos, ends, tile_expert, n_used, n_tiles = _route_plan(meta, counts, ne, tg)
                xs = _dispatch(x, pos, ends, n_tiles * tg, tg)
                ys = _grouped_ffn(xs, w['moe_w_in'][k], w['moe_w_out'][k], tile_expert, n_used,
                                  tg)
                x, xb = _moe_post(x, ys, pos, meta, *post_args)
    return (x.reshape(bsz, t, d), jnp.stack(new_lru_conv), jnp.stack(new_lru_h),
            jnp.stack(new_cm_conv))


_MATMUL_WEIGHTS = ('rglru_w_in', 'rglru_w_a', 'rglru_w_x', 'rglru_w_out', 'cm_w_pw1', 'cm_w_pw2',
                   'ffn_w_in', 'ffn_w_out', 'moe_w_in', 'moe_w_out', 'ple_w_proj', 'ple_w_gate')


def kernel(x_prompt, x_sample, p_prompt, p_sample, state_rglru_conv, state_rglru_h, state_conv_module, rglru_w_in, rglru_conv_w, rglru_conv_b, rglru_w_a, rglru_b_a, rglru_w_x, rglru_b_x, rglru_lambda, rglru_w_out, cm_w_pw1, cm_b_pw1, cm_dw_w, cm_dw_b, cm_ln_g, cm_ln_b, cm_w_pw2, cm_b_pw2, ffn_w_in, ffn_w_out, moe_w_router, moe_w_in, moe_w_out, ln_mix_g, ln_mix_b, ln_ffn_g, ln_ffn_b, ple_w_proj, ple_w_gate, ple_b_gate):
    w = dict(rglru_w_in=rglru_w_in, rglru_conv_w=rglru_conv_w, rglru_conv_b=rglru_conv_b,
             rglru_w_a=rglru_w_a, rglru_b_a=rglru_b_a, rglru_w_x=rglru_w_x, rglru_b_x=rglru_b_x,
             rglru_lambda=rglru_lambda, rglru_w_out=rglru_w_out, cm_w_pw1=cm_w_pw1,
             cm_b_pw1=cm_b_pw1, cm_dw_w=cm_dw_w, cm_dw_b=cm_dw_b, cm_ln_g=cm_ln_g,
             cm_ln_b=cm_ln_b, cm_w_pw2=cm_w_pw2, cm_b_pw2=cm_b_pw2, ffn_w_in=ffn_w_in,
             ffn_w_out=ffn_w_out, moe_w_router=moe_w_router, moe_w_in=moe_w_in,
             moe_w_out=moe_w_out, ln_mix_g=ln_mix_g, ln_mix_b=ln_mix_b, ln_ffn_g=ln_ffn_g,
             ln_ffn_b=ln_ffn_b, ple_w_proj=ple_w_proj, ple_w_gate=ple_w_gate,
             ple_b_gate=ple_b_gate)
    for name in _MATMUL_WEIGHTS:
        w[name] = w[name].astype(_BF)
    bp = x_prompt.shape[0]
    dt = x_prompt.dtype
    zero_conv = jnp.zeros((state_rglru_conv.shape[0], bp) + state_rglru_conv.shape[2:], dt)
    zero_h = jnp.zeros((state_rglru_h.shape[0], bp) + state_rglru_h.shape[2:], dt)
    zero_cm = jnp.zeros((state_conv_module.shape[0], bp) + state_conv_module.shape[2:], dt)
    y_p, conv_p, h_p, cm_p = _trunk(x_prompt, p_prompt, zero_conv, zero_h, zero_cm, w)
    y_s, conv_s, h_s, cm_s = _trunk(x_sample, p_sample, state_rglru_conv, state_rglru_h,
                                    state_conv_module, w)
    return (y_p, y_s, conv_p, h_p, cm_p, conv_s, h_s, cm_s)
```
